```python
import jax, jax.numpy as jnp
from jax import lax
import numpy as np

D_MODEL = 1024
BATCH = 2
SEQ = 8192
DEPTH = 1

CHUNK = 64
N_META = 16
Q_BLOCK = 128
HEAD_DIM = 64
N_HEADS_FOX = 8
N_HEADS_SB = 8
WIDTH_FOX = N_HEADS_FOX * HEAD_DIM
WIDTH_SB = N_HEADS_SB * HEAD_DIM
D_FF = 2816
CONV_WIDTH = 3
EPS = 1e-6

SPLIT_SIZES = (WIDTH_FOX, WIDTH_FOX, WIDTH_FOX, N_HEADS_FOX,
               WIDTH_SB, WIDTH_SB, WIDTH_SB, D_MODEL, D_MODEL)
PROJ_WIDTH = sum(SPLIT_SIZES)
SPLIT_POINTS = tuple(sum(SPLIT_SIZES[:i + 1]) for i in range(len(SPLIT_SIZES) - 1))

kernel_name = "hybrid_fox_stickbreak_convffn_block"


def rms_norm(x, g):
    xf = x.astype(jnp.float32)
    y = xf * lax.rsqrt(jnp.mean(xf * xf, axis=-1, keepdims=True) + EPS)
    return (y * g.astype(jnp.float32)).astype(x.dtype)


def _split_heads(t, n_heads):
    b, l, _ = t.shape
    return t.reshape(b, l, n_heads, HEAD_DIM).transpose(0, 2, 1, 3)


def _merge_heads(t):
    b, h, l, d = t.shape
    return t.transpose(0, 2, 1, 3).reshape(b, l, h * d)


def forgetting_attention(q, k, v, log_f):
    seq_len = q.shape[2]
    scale = HEAD_DIM ** -0.5
    c = jnp.cumsum(log_f, axis=-1)
    outs = []
    for start in range(0, seq_len, Q_BLOCK):
        end = min(start + Q_BLOCK, seq_len)
        s = jnp.einsum('bhqd,bhkd->bhqk', q[:, :, start:end], k[:, :, :end],
                       preferred_element_type=jnp.float32) * scale
        s = s + c[:, :, start:end, None] - c[:, :, None, :end]
        t_pos = jnp.arange(start, end)[:, None]
        s_pos = jnp.arange(end)[None, :]
        s = jnp.where(s_pos <= t_pos, s, -jnp.inf)
        p = jax.nn.softmax(s, axis=-1)
        outs.append(jnp.einsum('bhqk,bhkd->bhqd', p.astype(v.dtype), v[:, :, :end]))
    return jnp.concatenate(outs, axis=2)


def stick_breaking_attention(q, k, v):
    seq_len = q.shape[2]
    scale = HEAD_DIM ** -0.5
    outs = []
    for start in range(0, seq_len, Q_BLOCK):
        end = min(start + Q_BLOCK, seq_len)
        z = jnp.einsum('bhqd,bhkd->bhqk', q[:, :, start:end], k[:, :, :end],
                       preferred_element_type=jnp.float32) * scale
        t_pos = jnp.arange(start, end)[:, None]
        s_pos = jnp.arange(end)[None, :]
        causal = s_pos < t_pos
        log_keep = jnp.where(causal, jax.nn.log_sigmoid(-z), 0.0)
        later = lax.cumsum(log_keep, axis=3, reverse=True) - log_keep
        a = jnp.where(causal, jnp.exp(jax.nn.log_sigmoid(z) + later), 0.0)
        outs.append(jnp.einsum('bhqk,bhkd->bhqd', a.astype(v.dtype), v[:, :, :end]))
    return jnp.concatenate(outs, axis=2)


def causal_depthwise_conv(u, w, b):
    seq_len = u.shape[1]
    up = jnp.pad(u, ((0, 0), (CONV_WIDTH - 1, 0), (0, 0)))
    out = b.astype(u.dtype)
    for i in range(CONV_WIDTH):
        out = out + w[i] * up[:, i:i + seq_len]
    return out


def setup_inputs(seed: int = 0) -> dict:
    key = jax.random.key(seed)
    ks = jax.random.split(key, 13)
    x = jax.random.normal(ks[0], (BATCH, SEQ, D_MODEL), jnp.float32)
    meta_tokens = jax.random.normal(ks[1], (N_META, D_MODEL), jnp.float32)
    norm_gains = 1.0 + 0.05 * jax.random.normal(ks[2], (DEPTH, 4, D_MODEL), jnp.float32)
    w_in = jax.random.normal(ks[3], (DEPTH, D_MODEL, PROJ_WIDTH), jnp.float32) * D_MODEL ** -0.5
    b_forget = jax.random.uniform(ks[4], (DEPTH, N_HEADS_FOX), jnp.float32, minval=1.0, maxval=6.0)
    w_o_fox = jax.random.normal(ks[5], (DEPTH, WIDTH_FOX, D_MODEL), jnp.float32) * WIDTH_FOX ** -0.5
    w_o_sb = jax.random.normal(ks[6], (DEPTH, WIDTH_SB, D_MODEL), jnp.float32) * WIDTH_SB ** -0.5
    w_out = jax.random.normal(ks[7], (DEPTH, D_MODEL, D_MODEL), jnp.float32) * D_MODEL ** -0.5
    w_up = jax.random.normal(ks[8], (DEPTH, D_MODEL, 2 * D_FF), jnp.float32) * D_MODEL ** -0.5
    conv_w = jax.random.normal(ks[9], (DEPTH, CONV_WIDTH, 2 * D_FF), jnp.float32) * CONV_WIDTH ** -0.5
    conv_b = 0.02 * jax.random.normal(ks[10], (DEPTH, 2 * D_FF), jnp.float32)
    w_down = jax.random.normal(ks[11], (DEPTH, D_FF, D_MODEL), jnp.float32) * D_FF ** -0.5
    return {"x": x, "meta_tokens": meta_tokens, "norm_gains": norm_gains, "w_in": w_in,
            "b_forget": b_forget, "w_o_fox": w_o_fox, "w_o_sb": w_o_sb, "w_out": w_out,
            "w_up": w_up, "conv_w": conv_w, "conv_b": conv_b, "w_down": w_down}


def reference(x, meta_tokens, norm_gains, w_in, b_forget, w_o_fox, w_o_sb, w_out,
              w_up, conv_w, conv_b, w_down):
    batch = x.shape[0]
    meta = jnp.broadcast_to(meta_tokens[None].astype(x.dtype), (batch, N_META, D_MODEL))
    h = jnp.concatenate([meta, x], axis=1)
    for layer in range(DEPTH):
        xn = rms_norm(h, norm_gains[layer, 0])
        proj = xn @ w_in[layer]
        q_a, k_a, v_a, f_a, q_b, k_b, v_b, g_a, g_b = jnp.split(proj, SPLIT_POINTS, axis=-1)
        log_f = jax.nn.log_sigmoid((f_a + b_forget[layer]).astype(jnp.float32))
        o_a = forgetting_attention(_split_heads(q_a, N_HEADS_FOX), _split_heads(k_a, N_HEADS_FOX),
                                   _split_heads(v_a, N_HEADS_FOX), log_f.transpose(0, 2, 1))
        o_b = stick_breaking_attention(_split_heads(q_b, N_HEADS_SB), _split_heads(k_b, N_HEADS_SB),
                                       _split_heads(v_b, N_HEADS_SB))
        y_a = _merge_heads(o_a) @ w_o_fox[layer]
        y_b = _merge_heads(o_b) @ w_o_sb[layer]
        mixed = (jax.nn.sigmoid(g_a) * y_a + jax.nn.sigmoid(g_b) * y_b) @ w_out[layer]
        h = h + rms_norm(mixed, norm_gains[layer, 1])
        xn = rms_norm(h, norm_gains[layer, 2])
        u = causal_depthwise_conv(xn @ w_up[layer], conv_w[layer], conv_b[layer])
        u_gate, u_val = jnp.split(u, 2, axis=-1)
        ffn = (jax.nn.gelu(u_gate, approximate=True) * u_val) @ w_down[layer]
        h = h + rms_norm(ffn, norm_gains[layer, 3])
    return h[:, N_META:]
```

```python
import functools

import jax
import jax.numpy as jnp
from jax import lax
from jax.experimental import pallas as pl
from jax.experimental.pallas import tpu as pltpu

D_MODEL = 1024
N_META = 16
HEAD_DIM = 64
N_HEADS = 8
WIDTH = N_HEADS * HEAD_DIM
D_FF = 2816
CONV_WIDTH = 3
EPS = 1e-6

BLK = 256
PAD = BLK - N_META
LANES = 128
HALO = 8
FF_CHUNK = 256
NEG = -1e30
VMEM_LIMIT = 56 * 1024 * 1024

F32 = jnp.float32
BF16 = jnp.bfloat16


def _const_spec(shape):
    zeros = (0,) * len(shape)
    return pl.BlockSpec(shape, lambda *_: zeros, pipeline_mode=pl.Buffered(1))


def _rms_norm(v, g):
    ms = jnp.mean(v * v, axis=-1, keepdims=True)
    return v * lax.rsqrt(ms + EPS) * g


def _split3(v):
    hi = v.astype(BF16)
    r1 = v - hi.astype(F32)
    mid = r1.astype(BF16)
    lo = (r1 - mid.astype(F32)).astype(BF16)
    return hi, mid, lo


def _log_sigmoid(v):
    return jnp.minimum(v, 0.0) - jnp.log1p(jnp.exp(-jnp.abs(v)))


def _proj_kernel(x_ref, meta_ref, g_ref, wn_ref, wt_ref, bf_ref, tri_ref,
                 kf_ref, ks_ref, ga_ref, gb_ref, qft_ref, vft_ref, qst_ref, vst_ref,
                 ka_ref, ct_ref, carry_ref):
    t = pl.program_id(1)

    @pl.when(t == 0)
    def _():
        carry_ref[...] = jnp.zeros_like(carry_ref)

    h = jnp.where(t == 0, meta_ref[...], x_ref[0])
    xn = _rms_norm(h, g_ref[...]).astype(BF16)

    pn = jnp.dot(xn, wn_ref[...], preferred_element_type=F32)
    kf_ref[0] = pn[:, 0:WIDTH].astype(BF16)
    ks_ref[0] = pn[:, WIDTH:2 * WIDTH].astype(BF16)
    ga_ref[0] = pn[:, 2 * WIDTH:2 * WIDTH + D_MODEL].astype(BF16)
    gb_ref[0] = pn[:, 2 * WIDTH + D_MODEL:2 * WIDTH + 2 * D_MODEL].astype(BF16)

    pt = lax.dot_general(wt_ref[...], xn, (((1,), (1,)), ((), ())),
                         preferred_element_type=F32)
    qft_ref[0] = pt[0:WIDTH].astype(BF16)
    vft_ref[0] = pt[WIDTH:2 * WIDTH].astype(BF16)
    qst_ref[0] = pt[2 * WIDTH:3 * WIDTH].astype(BF16)
    vst_ref[0] = pt[3 * WIDTH:4 * WIDTH].astype(BF16)

    f = pn[:, 2 * WIDTH + 2 * D_MODEL:] + bf_ref[...]
    row = lax.broadcasted_iota(jnp.int32, (BLK, LANES), 0)
    lane = lax.broadcasted_iota(jnp.int32, (BLK, LANES), 1)
    valid = (lane < N_HEADS) & ((t > 0) | (row >= PAD))
    logf = jnp.where(valid, _log_sigmoid(f), 0.0)
    tri = tri_ref[...]
    c = carry_ref[...]
    for part in _split3(logf):
        c = c + jnp.dot(tri, part, preferred_element_type=F32)
    carry_ref[...] = c[BLK - 1:BLK, :]

    chi, cmid, clo = _split3(c)
    spread = (chi.astype(F32) + pltpu.roll(cmid.astype(F32), N_HEADS, axis=1)
              + pltpu.roll(clo.astype(F32), 2 * N_HEADS, axis=1))
    ones = jnp.where((lane >= 32) & (lane < 32 + 3 * N_HEADS), 1.0, 0.0)
    ka_ref[0] = (ones - spread).astype(BF16)
    ct_ref[0] = spread.T[0:32].astype(BF16)


def _proj_call(x, meta_blk, g0, wn, wt, bfp, tri, lp):
    b, seq, _ = x.shape
    nt = lp // BLK
    x_spec = pl.BlockSpec((1, BLK, D_MODEL), lambda bi, t: (bi, jnp.maximum(t - 1, 0), 0))
    row_spec = lambda w: pl.BlockSpec((1, BLK, w), lambda bi, t: (bi, t, 0))
    col_spec = lambda r: pl.BlockSpec((1, r, BLK), lambda bi, t: (bi, 0, t))
    nat = lambda w: jax.ShapeDtypeStruct((b, lp, w), BF16)
    tr = lambda r: jax.ShapeDtypeStruct((b, r, lp), BF16)
    return pl.pallas_call(
        _proj_kernel,
        grid=(b, nt),
        in_specs=[x_spec, _const_spec(meta_blk.shape), _const_spec(g0.shape), _const_spec(wn.shape),
                  _const_spec(wt.shape), _const_spec(bfp.shape), _const_spec(tri.shape)],
        out_specs=[row_spec(WIDTH), row_spec(WIDTH), row_spec(D_MODEL), row_spec(D_MODEL),
                   col_spec(WIDTH), col_spec(WIDTH), col_spec(WIDTH), col_spec(WIDTH),
                   row_spec(LANES), col_spec(32)],
        out_shape=[nat(WIDTH), nat(WIDTH), nat(D_MODEL), nat(D_MODEL),
                   tr(WIDTH), tr(WIDTH), tr(WIDTH), tr(WIDTH), nat(LANES), tr(32)],
        scratch_shapes=[pltpu.VMEM((1, LANES), F32)],
        compiler_params=pltpu.CompilerParams(
            dimension_semantics=("arbitrary", "arbitrary"), vmem_limit_bytes=VMEM_LIMIT),
        name="proj",
    )(x, meta_blk, g0, wn, wt, bfp, tri)


def _tile_mask(kb, qi, strict):
    s_idx = kb * BLK + lax.broadcasted_iota(jnp.int32, (BLK, BLK), 0)
    t_idx = qi * BLK + lax.broadcasted_iota(jnp.int32, (BLK, BLK), 1)
    causal = (s_idx < t_idx) if strict else (s_idx <= t_idx)
    return causal & (s_idx >= PAD)


def _sweep(tile, qi, init):
    carry = tile(qi, True, init)
    carry = lax.fori_loop(0, jnp.maximum(qi - 1, 0),
                          lambda i, c: tile(qi - 1 - i, False, c), carry)
    return lax.cond(qi > 0, lambda c: tile(0, True, c), lambda c: c, carry)


def _head_rows(qt2, hh):
    z = jnp.zeros((HEAD_DIM, BLK), BF16)
    q = qt2[hh * HEAD_DIM:(hh + 1) * HEAD_DIM]
    return jnp.concatenate([q, z] if hh == 0 else [z, q], axis=0)


def _fox_kernel(k2_ref, ka_ref, vt2_ref, qt2_ref, ct_ref, o_ref, ot_ref):
    p = pl.program_id(1)
    qi = pl.program_id(2)
    qt2 = qt2_ref[0]
    ct = ct_ref[0]
    r32 = lax.broadcasted_iota(jnp.int32, (32, BLK), 0)

    for hh in range(2):
        head = 2 * p + hh
        sel = (r32 < 3 * N_HEADS) & ((r32 % N_HEADS) == head)
        rhs = jnp.concatenate(
            [_head_rows(qt2, hh),
             jnp.where(sel, 1.0, 0.0).astype(BF16),
             jnp.where(sel, ct, jnp.zeros_like(ct)),
             jnp.zeros((64, BLK), BF16)], axis=0)

        def tile(kb, masked, carry, hh=hh, rhs=rhs):
            m, l, acc = carry
            off = pl.multiple_of(kb * BLK, BLK)
            lhs = jnp.concatenate([k2_ref[0, pl.ds(off, BLK), :], ka_ref[0, pl.ds(off, BLK), :]], axis=1)
            s = jnp.dot(lhs, rhs, preferred_element_type=F32)
            if masked:
                s = jnp.where(_tile_mask(kb, qi, strict=False), s, NEG)
            m_new = jnp.maximum(m, jnp.max(s, axis=0, keepdims=True))
            alpha = jnp.exp(m - m_new)
            pr = jnp.exp(s - m_new)
            l = alpha * l + jnp.sum(pr, axis=0, keepdims=True)
            vt = vt2_ref[0, hh * HEAD_DIM:(hh + 1) * HEAD_DIM, pl.ds(off, BLK)]
            acc = alpha * acc + jnp.dot(vt, pr.astype(BF16), preferred_element_type=F32)
            return m_new, l, acc

        init = (jnp.full((1, BLK), NEG, F32), jnp.zeros((1, BLK), F32), jnp.zeros((HEAD_DIM, BLK), F32))
        _, l, acc = _sweep(tile, qi, init)
        ot_ref[hh * HEAD_DIM:(hh + 1) * HEAD_DIM, :] = acc / l

    o_ref[0] = ot_ref[...].T.astype(BF16)


def _sb_kernel(k2_ref, vt2_ref, qt2_ref, tt_ref, o_ref, ot_ref):
    qi = pl.program_id(2)
    qt2 = qt2_ref[0]

    for hh in range(2):
        rhs = _head_rows(qt2, hh)

        def tile(kb, masked, carry, hh=hh, rhs=rhs):
            cr, acc = carry
            off = pl.multiple_of(kb * BLK, BLK)
            z = jnp.dot(k2_ref[0, pl.ds(off, BLK), :], rhs, preferred_element_type=F32)
            lk = jnp.minimum(-z, 0.0) - jnp.log1p(jnp.exp(-jnp.abs(z)))
            if masked:
                ok = _tile_mask(kb, qi, strict=True)
                lk = jnp.where(ok, lk, 0.0)
            hi = lk.astype(BF16)
            lo = (lk - hi.astype(F32)).astype(BF16)
            cum = jnp.dot(tt_ref[...], jnp.concatenate([hi, lo], axis=0), preferred_element_type=F32)
            a = jnp.exp(z + cum + cr)
            if masked:
                a = jnp.where(ok, a, 0.0)
            vt = vt2_ref[0, hh * HEAD_DIM:(hh + 1) * HEAD_DIM, pl.ds(off, BLK)]
            acc = acc + jnp.dot(vt, a.astype(BF16), preferred_element_type=F32)
            return cr + cum[0:1, :], acc

        init = (jnp.zeros((1, BLK), F32), jnp.zeros((HEAD_DIM, BLK), F32))
        _, acc = _sweep(tile, qi, init)
        ot_ref[hh * HEAD_DIM:(hh + 1) * HEAD_DIM, :] = acc

    o_ref[0] = ot_ref[...].T.astype(BF16)


def _attn_call(kernel, name, k, vt, qt, extra_in, extra_specs, lp):
    b = k.shape[0]
    nq = lp // BLK
    npair = N_HEADS // 2
    k_spec = pl.BlockSpec((1, lp, LANES), lambda bi, p, qi: (bi, 0, p))
    vt_spec = pl.BlockSpec((1, LANES, lp), lambda bi, p, qi: (bi, p, 0))
    qt_spec = pl.BlockSpec((1, LANES, BLK), lambda bi, p, qi: (bi, p, qi))
    in_specs, args = extra_specs(k_spec, vt_spec, qt_spec), extra_in(k, vt, qt)
    return pl.pallas_call(
        kernel,
        grid=(b, npair, nq),
        in_specs=in_specs,
        out_specs=pl.BlockSpec((1, BLK, LANES), lambda bi, p, qi: (bi, qi, p)),
        out_shape=jax.ShapeDtypeStruct((b, lp, WIDTH), BF16),
        scratch_shapes=[pltpu.VMEM((LANES, BLK), F32)],
        compiler_params=pltpu.CompilerParams(
            dimension_semantics=("arbitrary", "arbitrary", "arbitrary"), vmem_limit_bytes=VMEM_LIMIT),
        name=name,
    )(*args)


def _fox_call(kf, ka, vft, qft, ct, lp):
    ka_spec = pl.BlockSpec((1, lp, LANES), lambda bi, p, qi: (bi, 0, 0))
    ct_spec = pl.BlockSpec((1, 32, BLK), lambda bi, p, qi: (bi, 0, qi))
    return _attn_call(_fox_kernel, "fox", kf, vft, qft,
                      lambda k, vt, qt: (k, ka, vt, qt, ct),
                      lambda ks, vs, qs: [ks, ka_spec, vs, qs, ct_spec], lp)


def _sb_call(ks, vst, qst, tritri, lp):
    return _attn_call(_sb_kernel, "sb", ks, vst, qst,
                      lambda k, vt, qt: (k, vt, qt, tritri),
                      lambda kspec, vs, qs: [kspec, vs, qs, _const_spec(tritri.shape)], lp)


def _post_kernel(x_ref, meta_ref, of_ref, os_ref, ga_ref, gb_ref, wof_ref, wos_ref, wout_ref, g_ref,
                 h1_ref):
    t = pl.program_id(1)
    h = jnp.where(t == 0, meta_ref[...], x_ref[0])
    ya = jnp.dot(of_ref[0], wof_ref[...], preferred_element_type=F32)
    yb = jnp.dot(os_ref[0], wos_ref[...], preferred_element_type=F32)
    merged = (jax.nn.sigmoid(ga_ref[0].astype(F32)) * ya + jax.nn.sigmoid(gb_ref[0].astype(F32)) * yb)
    mixed = jnp.dot(merged.astype(BF16), wout_ref[...], preferred_element_type=F32)
    h1 = h + _rms_norm(mixed, g_ref[...])
    row = lax.broadcasted_iota(jnp.int32, (BLK, 1), 0)
    h1_ref[0] = jnp.where((t > 0) | (row >= PAD), h1, 0.0)


def _post_call(x, meta_blk, o_f, o_s, ga, gb, wof, wos, wout, g1, lp):
    b = x.shape[0]
    nt = lp // BLK
    x_spec = pl.BlockSpec((1, BLK, D_MODEL), lambda bi, t: (bi, jnp.maximum(t - 1, 0), 0))
    row_spec = lambda w: pl.BlockSpec((1, BLK, w), lambda bi, t: (bi, t, 0))
    return pl.pallas_call(
        _post_kernel,
        grid=(b, nt),
        in_specs=[x_spec, _const_spec(meta_blk.shape), row_spec(WIDTH), row_spec(WIDTH),
                  row_spec(D_MODEL), row_spec(D_MODEL), _const_spec(wof.shape), _const_spec(wos.shape),
                  _const_spec(wout.shape), _const_spec(g1.shape)],
        out_specs=row_spec(D_MODEL),
        out_shape=jax.ShapeDtypeStruct((b, lp, D_MODEL), F32),
        compiler_params=pltpu.CompilerParams(
            dimension_semantics=("arbitrary", "arbitrary"), vmem_limit_bytes=VMEM_LIMIT),
        name="post",
    )(x, meta_blk, o_f, o_s, ga, gb, wof, wos, wout, g1)


def _ffn_kernel(h1_ref, g2_ref, g3_ref, wup_ref, cw_ref, cb_ref, wdn_ref, out_ref, u_ref):
    t = pl.program_id(1)

    @pl.when(t == 0)
    def _():
        u_ref[0:HALO, :] = jnp.zeros((HALO, 2 * D_FF), F32)

    @pl.when(t > 0)
    def _():
        u_ref[0:HALO, :] = u_ref[BLK:BLK + HALO, :]

    h1 = h1_ref[0]
    xn = _rms_norm(h1, g2_ref[...]).astype(BF16)
    u_ref[HALO:HALO + BLK, :] = jnp.dot(xn, wup_ref[...], preferred_element_type=F32)

    def conv(cols):
        out = cb_ref[:, cols]
        for i in range(CONV_WIDTH):
            out = out + cw_ref[i:i + 1, cols] * u_ref[HALO - (CONV_WIDTH - 1) + i:HALO - (CONV_WIDTH - 1) + i + BLK, cols]
        return out

    ffn = jnp.zeros((BLK, D_MODEL), F32)
    for j in range(D_FF // FF_CHUNK):
        gate = conv(slice(j * FF_CHUNK, (j + 1) * FF_CHUNK))
        val = conv(slice(D_FF + j * FF_CHUNK, D_FF + (j + 1) * FF_CHUNK))
        act = (jax.nn.gelu(gate, approximate=True) * val).astype(BF16)
        ffn = ffn + jnp.dot(act, wdn_ref[j * FF_CHUNK:(j + 1) * FF_CHUNK, :], preferred_element_type=F32)

    out_ref[0] = h1 + _rms_norm(ffn, g3_ref[...])


def _ffn_call(h1, g2, g3, wup, cw, cb, wdn, seq):
    b, lp, _ = h1.shape
    nt = lp // BLK
    return pl.pallas_call(
        _ffn_kernel,
        grid=(b, nt),
        in_specs=[pl.BlockSpec((1, BLK, D_MODEL), lambda bi, t: (bi, t, 0)),
                  _const_spec(g2.shape), _const_spec(g3.shape), _const_spec(wup.shape),
                  _const_spec(cw.shape), _const_spec(cb.shape), _const_spec(wdn.shape)],
        out_specs=pl.BlockSpec((1, BLK, D_MODEL), lambda bi, t: (bi, jnp.maximum(t - 1, 0), 0)),
        out_shape=jax.ShapeDtypeStruct((b, seq, D_MODEL), F32),
        scratch_shapes=[pltpu.VMEM((HALO + BLK, 2 * D_FF), F32)],
        compiler_params=pltpu.CompilerParams(
            dimension_semantics=("arbitrary", "arbitrary"), vmem_limit_bytes=VMEM_LIMIT),
        name="ffn",
    )(h1, g2, g3, wup, cw, cb, wdn)


def kernel(x, meta_tokens, norm_gains, w_in, b_forget, w_o_fox, w_o_sb, w_out, w_up, conv_w, conv_b, w_down):
    b, seq, d = x.shape
    assert d == D_MODEL and seq % BLK == 0 and D_FF % FF_CHUNK == 0
    assert norm_gains.shape[0] == 1, "single-layer block"
    lp = BLK + seq
    scale = HEAD_DIM ** -0.5

    w = w_in[0]
    o = 0
    parts = {}
    for name, width in (("qf", WIDTH), ("kf", WIDTH), ("vf", WIDTH), ("f", N_HEADS),
                        ("qs", WIDTH), ("ks", WIDTH), ("vs", WIDTH), ("ga", D_MODEL), ("gb", D_MODEL)):
        parts[name] = w[:, o:o + width]
        o += width
    f_pad = jnp.pad(parts["f"], ((0, 0), (0, LANES - N_HEADS)))
    wn = jnp.concatenate([parts["kf"], parts["ks"], parts["ga"], parts["gb"], f_pad], axis=1).astype(BF16)
    wt = jnp.concatenate([parts["qf"] * scale, parts["vf"], parts["qs"] * scale, parts["vs"]], axis=1).T.astype(BF16)
    bfp = jnp.pad(b_forget[0], (0, LANES - N_HEADS)).reshape(1, LANES)

    meta_blk = jnp.concatenate([jnp.zeros((PAD, D_MODEL), x.dtype), meta_tokens.astype(x.dtype)], axis=0)
    gains = norm_gains[0].reshape(4, 1, D_MODEL)

    idx = jnp.arange(BLK)
    tri_low = (idx[None, :] <= idx[:, None]).astype(BF16)
    tri_up = (idx[None, :] >= idx[:, None]).astype(BF16)
    tritri = jnp.concatenate([tri_up, tri_up], axis=1)

    kf, ks, ga, gb, qft, vft, qst, vst, ka, ct = _proj_call(x, meta_blk, gains[0], wn, wt, bfp, tri_low, lp)
    o_f = _fox_call(kf, ka, vft, qft, ct, lp)
    o_s = _sb_call(ks, vst, qst, tritri, lp)
    h1 = _post_call(x, meta_blk, o_f, o_s, ga, gb, w_o_fox[0].astype(BF16), w_o_sb[0].astype(BF16),
                    w_out[0].astype(BF16), gains[1], lp)
    return _ffn_call(h1, gains[2], gains[3], w_up[0].astype(BF16), conv_w[0], conv_b[0].reshape(1, 2 * D_FF),
                     w_down[0].astype(BF16), seq)
```

```python
import jax
import jax.numpy as jnp
from jax import lax
from jax.experimental import pallas as pl
from jax.experimental.pallas import tpu as pltpu

D_MODEL = 1024
N_META = 16
HEAD_DIM = 64
N_HEADS = 8
WIDTH = N_HEADS * HEAD_DIM
D_FF = 2816
CONV_WIDTH = 3
EPS = 1e-6

BLK = 256
QBLK = 512
KPQ = QBLK // BLK
PAD = QBLK - N_META
NPRE = QBLK // BLK
META_PAD = BLK - N_META
LANES = 128
HALO = 8
FF_CHUNK = 256
NEG = -1e30
SB_WIN = 3
EXP_UNDERFLOW = -104.0
VMEM_LIMIT = 56 * 1024 * 1024

F32 = jnp.float32
BF16 = jnp.bfloat16


def _const_spec(shape):
    zeros = (0,) * len(shape)
    return pl.BlockSpec(shape, lambda *_: zeros, pipeline_mode=pl.Buffered(1))


def _rms_norm(v, g):
    ms = jnp.mean(v * v, axis=-1, keepdims=True)
    return v * lax.rsqrt(ms + EPS) * g


def _split3(v):
    hi = v.astype(BF16)
    r1 = v - hi.astype(F32)
    mid = r1.astype(BF16)
    lo = (r1 - mid.astype(F32)).astype(BF16)
    return hi, mid, lo


def _log_sigmoid(v):
    return jnp.minimum(v, 0.0) - jnp.log1p(jnp.exp(-jnp.abs(v)))


def _tile_input(t, x_ref, meta_ref):
    h = jnp.where(t == NPRE - 1, meta_ref[...], x_ref[0])
    return jnp.where(t < NPRE - 1, 0.0, h)


def _valid_rows(t, shape):
    row = lax.broadcasted_iota(jnp.int32, shape, 0)
    return (t >= NPRE) | ((t == NPRE - 1) & (row >= META_PAD))


def _x_spec():
    return pl.BlockSpec((1, BLK, D_MODEL), lambda bi, t: (bi, jnp.maximum(t - NPRE, 0), 0))


def _proj_kernel(x_ref, meta_ref, g_ref, wn_ref, wt_ref, bf_ref, tri_ref,
                 kf_ref, ks_ref, ga_ref, gb_ref, qft_ref, vft_ref, qst_ref, vst_ref,
                 ka_ref, ct_ref, carry_ref):
    t = pl.program_id(1)

    @pl.when(t == 0)
    def _():
        carry_ref[...] = jnp.zeros_like(carry_ref)

    h = _tile_input(t, x_ref, meta_ref)
    xn = _rms_norm(h, g_ref[...]).astype(BF16)

    pn = jnp.dot(xn, wn_ref[...], preferred_element_type=F32)
    kf_ref[0] = pn[:, 0:WIDTH].astype(BF16)
    ks_ref[0] = pn[:, WIDTH:2 * WIDTH].astype(BF16)
    ga_ref[0] = pn[:, 2 * WIDTH:2 * WIDTH + D_MODEL].astype(BF16)
    gb_ref[0] = pn[:, 2 * WIDTH + D_MODEL:2 * WIDTH + 2 * D_MODEL].astype(BF16)

    pt = lax.dot_general(wt_ref[...], xn, (((1,), (1,)), ((), ())),
                         preferred_element_type=F32)
    qft_ref[0] = pt[0:WIDTH].astype(BF16)
    vft_ref[0] = pt[WIDTH:2 * WIDTH].astype(BF16)
    qst_ref[0] = pt[2 * WIDTH:3 * WIDTH].astype(BF16)
    vst_ref[0] = pt[3 * WIDTH:4 * WIDTH].astype(BF16)

    f = pn[:, 2 * WIDTH + 2 * D_MODEL:] + bf_ref[...]
    lane = lax.broadcasted_iota(jnp.int32, (BLK, LANES), 1)
    valid = (lane < N_HEADS) & _valid_rows(t, (BLK, LANES))
    logf = jnp.where(valid, _log_sigmoid(f), 0.0)
    tri = tri_ref[...]
    c = carry_ref[...]
    for part in _split3(logf):
        c = c + jnp.dot(tri, part, preferred_element_type=F32)
    carry_ref[...] = c[BLK - 1:BLK, :]

    chi, cmid, clo = _split3(c)
    spread = (chi.astype(F32) + pltpu.roll(cmid.astype(F32), N_HEADS, axis=1)
              + pltpu.roll(clo.astype(F32), 2 * N_HEADS, axis=1))
    ones = jnp.where((lane >= 32) & (lane < 32 + 3 * N_HEADS), 1.0, 0.0)
    ka_ref[0] = (ones - spread).astype(BF16)
    ct_ref[0] = spread.T[0:32].astype(BF16)


def _proj_call(x, meta_blk, g0, wn, wt, bfp, tri, lp):
    b = x.shape[0]
    nt = lp // BLK
    row_spec = lambda w: pl.BlockSpec((1, BLK, w), lambda bi, t: (bi, t, 0))
    col_spec = lambda r: pl.BlockSpec((1, r, BLK), lambda bi, t: (bi, 0, t))
    nat = lambda w: jax.ShapeDtypeStruct((b, lp, w), BF16)
    tr = lambda r: jax.ShapeDtypeStruct((b, r, lp), BF16)
    return pl.pallas_call(
        _proj_kernel,
        grid=(b, nt),
        in_specs=[_x_spec(), _const_spec(meta_blk.shape), _const_spec(g0.shape), _const_spec(wn.shape),
                  _const_spec(wt.shape), _const_spec(bfp.shape), _const_spec(tri.shape)],
        out_specs=[row_spec(WIDTH), row_spec(WIDTH), row_spec(D_MODEL), row_spec(D_MODEL),
                   col_spec(WIDTH), col_spec(WIDTH), col_spec(WIDTH), col_spec(WIDTH),
                   row_spec(LANES), col_spec(32)],
        out_shape=[nat(WIDTH), nat(WIDTH), nat(D_MODEL), nat(D_MODEL),
                   tr(WIDTH), tr(WIDTH), tr(WIDTH), tr(WIDTH), nat(LANES), tr(32)],
        scratch_shapes=[pltpu.VMEM((1, LANES), F32)],
        compiler_params=pltpu.CompilerParams(
            dimension_semantics=("arbitrary", "arbitrary"), vmem_limit_bytes=VMEM_LIMIT),
        name="proj",
    )(x, meta_blk, g0, wn, wt, bfp, tri)


def _tile_mask(kb, qi, strict):
    s_idx = kb * BLK + lax.broadcasted_iota(jnp.int32, (BLK, QBLK), 0)
    t_idx = qi * QBLK + lax.broadcasted_iota(jnp.int32, (BLK, QBLK), 1)
    causal = (s_idx < t_idx) if strict else (s_idx <= t_idx)
    return causal & (s_idx >= PAD)


def _head_rows(qt2, hh):
    z = jnp.zeros((HEAD_DIM, QBLK), BF16)
    q = qt2[hh * HEAD_DIM:(hh + 1) * HEAD_DIM]
    return jnp.concatenate([q, z] if hh == 0 else [z, q], axis=0)


def _fox_kernel(k2_ref, ka_ref, vt2_ref, qt2_ref, ct_ref, o_ref, s_ref, p_ref, acc_ref):
    assert KPQ == 2 and PAD // BLK == 1
    p = pl.program_id(1)
    qi = pl.program_id(2)
    qt2 = qt2_ref[0]
    ct = ct_ref[0]
    r32 = lax.broadcasted_iota(jnp.int32, (32, QBLK), 0)

    rhs = []
    for hh in range(2):
        sel = (r32 < 3 * N_HEADS) & ((r32 % N_HEADS) == 2 * p + hh)
        rhs.append(jnp.concatenate(
            [_head_rows(qt2, hh),
             jnp.where(sel, 1.0, 0.0).astype(BF16),
             jnp.where(sel, ct, jnp.zeros_like(ct)),
             jnp.zeros((64, QBLK), BF16)], axis=0))

    def scores(kb, masked, wb):
        off = pl.multiple_of(kb * BLK, BLK)
        lhs = jnp.concatenate([k2_ref[0, pl.ds(off, BLK), :], ka_ref[0, pl.ds(off, BLK), :]], axis=1)
        ok = _tile_mask(kb, qi, strict=False) if masked else None
        for hh in range(2):
            s = jnp.dot(lhs, rhs[hh], preferred_element_type=F32)
            s_ref[wb, hh] = jnp.where(ok, s, NEG) if masked else s

    def softmax(rb, wb, stats):
        out = []
        for hh in range(2):
            m, l, _ = stats[hh]
            s = s_ref[rb, hh]
            m_new = jnp.maximum(m, jnp.max(s, axis=0, keepdims=True))
            alpha = jnp.exp(m - m_new)
            pr = jnp.exp(s - m_new)
            p_ref[wb, hh] = pr.astype(BF16)
            out.append((m_new, alpha * l + jnp.sum(pr, axis=0, keepdims=True), alpha))
        return tuple(out)

    def values(kb, rb, stats):
        off = pl.multiple_of(kb * BLK, BLK)
        for hh in range(2):
            vt = vt2_ref[0, hh * HEAD_DIM:(hh + 1) * HEAD_DIM, pl.ds(off, BLK)]
            acc_ref[hh] = stats[hh][2] * acc_ref[hh] + jnp.dot(vt, p_ref[rb, hh], preferred_element_type=F32)

    def step(kb, masked, rb, wb, stats):
        scores(kb, masked, wb)
        new = softmax(rb, wb, stats)
        values(kb + 2, rb, stats)
        return new

    acc_ref[...] = jnp.zeros_like(acc_ref)
    one = (jnp.full((1, QBLK), NEG, F32), jnp.zeros((1, QBLK), F32), jnp.ones((1, QBLK), F32))
    top = KPQ * qi + 1

    def first_block(stats):
        scores(top, True, 0)
        stats = softmax(0, 1, stats)
        values(top, 1, stats)
        return stats

    def pipelined(stats):
        scores(top, True, 0)
        scores(top - 1, True, 1)
        stats = softmax(0, 1, stats)

        def two_steps(i, st):
            kb = top - 2 - 2 * i
            st = step(kb, False, 1, 0, st)
            return step(kb - 1, False, 0, 1, st)

        stats = lax.fori_loop(0, qi - 1, two_steps, stats)
        stats = step(1, True, 1, 0, stats)
        last = softmax(0, 1, stats)
        values(2, 0, stats)
        values(1, 1, last)
        return last

    final = lax.cond(qi == 0, first_block, pipelined, (one, one))
    rows = [acc_ref[hh] / final[hh][1] for hh in range(2)]
    o_ref[0] = jnp.concatenate(rows, axis=0).T.astype(BF16)


def _sb_kernel(k2_ref, vt2_ref, qt2_ref, tt_ref, o_ref):
    assert KPQ == 2 and SB_WIN == 3 and PAD // BLK == 1
    qi = pl.program_id(2)
    qt2 = qt2_ref[0]
    rhs = [_head_rows(qt2, hh) for hh in range(2)]
    tt = tt_ref[...]

    def log_keep(z):
        return jnp.minimum(-z, 0.0) - jnp.log(1.0 + jnp.exp(-jnp.abs(z)))

    def suffix_sum(lk):
        hi = lk.astype(BF16)
        lo = (lk - hi.astype(F32)).astype(BF16)
        return jnp.dot(tt, jnp.concatenate([hi, lo], axis=0), preferred_element_type=F32)

    def block(z, ok, cr):
        lk = log_keep(z)
        if ok is not None:
            lk = jnp.where(ok, lk, 0.0)
        cum = suffix_sum(lk)
        a = jnp.exp(z + cum + cr)
        if ok is not None:
            a = jnp.where(ok, a, 0.0)
        return a.astype(BF16), cr + cum[0:1, :]

    top = KPQ * qi + 1
    half_ok = _tile_mask(top, qi, strict=True)[:, BLK:]
    zeros_a = jnp.zeros((BLK, BLK), BF16)
    zeros_c = jnp.zeros((1, BLK), F32)

    def upper_half(hh):
        off = pl.multiple_of(top * BLK, BLK)
        z = jnp.dot(k2_ref[0, pl.ds(off, BLK), :], rhs[hh][:, BLK:], preferred_element_type=F32)
        a, cr = block(z, half_ok, zeros_c)
        return jnp.concatenate([zeros_a, a], axis=1), jnp.concatenate([zeros_c, cr], axis=1)

    def first_block(_):
        out = []
        for hh in range(2):
            a, cr = upper_half(hh)
            off = pl.multiple_of(top * BLK, BLK)
            vt = vt2_ref[0, hh * HEAD_DIM:(hh + 1) * HEAD_DIM, pl.ds(off, BLK)]
            out.append((cr, jnp.dot(vt, a, preferred_element_type=F32)))
        return tuple(out)

    def window(pad_keys):
        def run(_):
            off0 = pl.multiple_of((top - 2) * BLK, BLK)
            kwin = k2_ref[0, pl.ds(off0, 2 * BLK), :]
            ok1 = _tile_mask(top - 1, qi, strict=True)
            ok0 = _tile_mask(top - 2, qi, strict=True) if pad_keys else None
            out = []
            for hh in range(2):
                z = jnp.dot(kwin, rhs[hh], preferred_element_type=F32)
                a2, cr = upper_half(hh)
                a1, cr = block(z[BLK:], ok1, cr)
                a0, cr = block(z[:BLK], ok0, cr)
                vt = vt2_ref[0, hh * HEAD_DIM:(hh + 1) * HEAD_DIM, pl.ds(off0, SB_WIN * BLK)]
                acc = jnp.dot(vt, jnp.concatenate([a0, a1, a2], axis=0), preferred_element_type=F32)
                out.append((cr, acc))
            return tuple(out)
        return run

    state = lax.switch(jnp.minimum(qi, 2), [first_block, window(True), window(False)], 0)

    def tile(kb, masked, st):
        off = pl.multiple_of(kb * BLK, BLK)
        k2 = k2_ref[0, pl.ds(off, BLK), :]
        ok = _tile_mask(kb, qi, strict=True) if masked else None
        out = []
        for hh in range(2):
            cr, acc = st[hh]
            a, cr = block(jnp.dot(k2, rhs[hh], preferred_element_type=F32), ok, cr)
            vt = vt2_ref[0, hh * HEAD_DIM:(hh + 1) * HEAD_DIM, pl.ds(off, BLK)]
            out.append((cr, acc + jnp.dot(vt, a, preferred_element_type=F32)))
        return tuple(out)

    def live(st):
        return jnp.maximum(jnp.max(st[0][0]), jnp.max(st[1][0])) >= EXP_UNDERFLOW

    kb, state = lax.while_loop(lambda c: (c[0] >= 2) & live(c[1]),
                               lambda c: (c[0] - 1, tile(c[0], False, c[1])),
                               (top - SB_WIN, state))
    state = lax.cond((kb == 1) & live(state), lambda st: tile(1, True, st), lambda st: st, state)
    o_ref[0] = jnp.concatenate([state[0][1], state[1][1]], axis=0).T.astype(BF16)


def _attn_call(kernel, name, args, in_specs, scratch, b, lp):
    return pl.pallas_call(
        kernel,
        grid=(b, N_HEADS // 2, lp // QBLK),
        in_specs=in_specs,
        out_specs=pl.BlockSpec((1, QBLK, LANES), lambda bi, p, qi: (bi, qi, p)),
        out_shape=jax.ShapeDtypeStruct((b, lp, WIDTH), BF16),
        scratch_shapes=scratch,
        compiler_params=pltpu.CompilerParams(
            dimension_semantics=("arbitrary", "arbitrary", "arbitrary"), vmem_limit_bytes=VMEM_LIMIT),
        name=name,
    )(*args)


def _pair_specs(lp):
    k_spec = pl.BlockSpec((1, lp, LANES), lambda bi, p, qi: (bi, 0, p))
    vt_spec = pl.BlockSpec((1, LANES, lp), lambda bi, p, qi: (bi, p, 0))
    qt_spec = pl.BlockSpec((1, LANES, QBLK), lambda bi, p, qi: (bi, p, qi))
    return k_spec, vt_spec, qt_spec


def _fox_call(kf, ka, vft, qft, ct, lp):
    k_spec, vt_spec, qt_spec = _pair_specs(lp)
    ka_spec = pl.BlockSpec((1, lp, LANES), lambda bi, p, qi: (bi, 0, 0))
    ct_spec = pl.BlockSpec((1, 32, QBLK), lambda bi, p, qi: (bi, 0, qi))
    scratch = [pltpu.VMEM((2, 2, BLK, QBLK), F32),
               pltpu.VMEM((2, 2, BLK, QBLK), BF16),
               pltpu.VMEM((2, HEAD_DIM, QBLK), F32)]
    return _attn_call(_fox_kernel, "fox", (kf, ka, vft, qft, ct),
                      [k_spec, ka_spec, vt_spec, qt_spec, ct_spec], scratch, kf.shape[0], lp)


def _sb_call(ks, vst, qst, tritri, lp):
    k_spec, vt_spec, qt_spec = _pair_specs(lp)
    return _attn_call(_sb_kernel, "sb", (ks, vst, qst, tritri),
                      [k_spec, vt_spec, qt_spec, _const_spec(tritri.shape)], [], ks.shape[0], lp)


def _post_kernel(x_ref, meta_ref, of_ref, os_ref, ga_ref, gb_ref, wof_ref, wos_ref, wout_ref, g_ref,
                 h1_ref):
    t = pl.program_id(1)
    h = _tile_input(t, x_ref, meta_ref)
    ya = jnp.dot(of_ref[0], wof_ref[...], preferred_element_type=F32)
    yb = jnp.dot(os_ref[0], wos_ref[...], preferred_element_type=F32)
    merged = (jax.nn.sigmoid(ga_ref[0].astype(F32)) * ya + jax.nn.sigmoid(gb_ref[0].astype(F32)) * yb)
    mixed = jnp.dot(merged.astype(BF16), wout_ref[...], preferred_element_type=F32)
    h1 = h + _rms_norm(mixed, g_ref[...])
    h1_ref[0] = jnp.where(_valid_rows(t, (BLK, 1)), h1, 0.0)


def _post_call(x, meta_blk, o_f, o_s, ga, gb, wof, wos, wout, g1, lp):
    b = x.shape[0]
    nt = lp // BLK
    row_spec = lambda w: pl.BlockSpec((1, BLK, w), lambda bi, t: (bi, t, 0))
    return pl.pallas_call(
        _post_kernel,
        grid=(b, nt),
        in_specs=[_x_spec(), _const_spec(meta_blk.shape), row_spec(WIDTH), row_spec(WIDTH),
                  row_spec(D_MODEL), row_spec(D_MODEL), _const_spec(wof.shape), _const_spec(wos.shape),
                  _const_spec(wout.shape), _const_spec(g1.shape)],
        out_specs=row_spec(D_MODEL),
        out_shape=jax.ShapeDtypeStruct((b, lp, D_MODEL), F32),
        compiler_params=pltpu.CompilerParams(
            dimension_semantics=("arbitrary", "arbitrary"), vmem_limit_bytes=VMEM_LIMIT),
        name="post",
    )(x, meta_blk, o_f, o_s, ga, gb, wof, wos, wout, g1)


def _ffn_kernel(h1_ref, g2_ref, g3_ref, wup_ref, cw_ref, cb_ref, wdn_ref, out_ref, u_ref):
    t = pl.program_id(1)

    @pl.when(t == 0)
    def _():
        u_ref[0:HALO, :] = jnp.zeros((HALO, 2 * D_FF), F32)

    @pl.when(t > 0)
    def _():
        u_ref[0:HALO, :] = u_ref[BLK:BLK + HALO, :]

    h1 = h1_ref[0]
    xn = _rms_norm(h1, g2_ref[...]).astype(BF16)
    u_ref[HALO:HALO + BLK, :] = jnp.dot(xn, wup_ref[...], preferred_element_type=F32)

    def conv(cols):
        out = cb_ref[:, cols]
        for i in range(CONV_WIDTH):
            lo = HALO - (CONV_WIDTH - 1) + i
            out = out + cw_ref[i:i + 1, cols] * u_ref[lo:lo + BLK, cols]
        return out

    ffn = jnp.zeros((BLK, D_MODEL), F32)
    for j in range(D_FF // FF_CHUNK):
        gate = conv(slice(j * FF_CHUNK, (j + 1) * FF_CHUNK))
        val = conv(slice(D_FF + j * FF_CHUNK, D_FF + (j + 1) * FF_CHUNK))
        act = (jax.nn.gelu(gate, approximate=True) * val).astype(BF16)
        ffn = ffn + jnp.dot(act, wdn_ref[j * FF_CHUNK:(j + 1) * FF_CHUNK, :], preferred_element_type=F32)

    out_ref[0] = h1 + _rms_norm(ffn, g3_ref[...])


def _ffn_call(h1, g2, g3, wup, cw, cb, wdn, seq):
    b, lp, _ = h1.shape
    nt = lp // BLK
    return pl.pallas_call(
        _ffn_kernel,
        grid=(b, nt),
        in_specs=[pl.BlockSpec((1, BLK, D_MODEL), lambda bi, t: (bi, t, 0)),
                  _const_spec(g2.shape), _const_spec(g3.shape), _const_spec(wup.shape),
                  _const_spec(cw.shape), _const_spec(cb.shape), _const_spec(wdn.shape)],
        out_specs=pl.BlockSpec((1, BLK, D_MODEL), lambda bi, t: (bi, jnp.maximum(t - NPRE, 0), 0)),
        out_shape=jax.ShapeDtypeStruct((b, seq, D_MODEL), F32),
        scratch_shapes=[pltpu.VMEM((HALO + BLK, 2 * D_FF), F32)],
        compiler_params=pltpu.CompilerParams(
            dimension_semantics=("arbitrary", "arbitrary"), vmem_limit_bytes=VMEM_LIMIT),
        name="ffn",
    )(h1, g2, g3, wup, cw, cb, wdn)


def kernel(x, meta_tokens, norm_gains, w_in, b_forget, w_o_fox, w_o_sb, w_out, w_up, conv_w, conv_b, w_down):
    b, seq, d = x.shape
    assert d == D_MODEL and seq % QBLK == 0 and D_FF % FF_CHUNK == 0
    assert norm_gains.shape[0] == 1, "single-layer block"
    lp = QBLK + seq
    scale = HEAD_DIM ** -0.5

    w = w_in[0]
    o = 0
    parts = {}
    for name, width in (("qf", WIDTH), ("kf", WIDTH), ("vf", WIDTH), ("f", N_HEADS),
                        ("qs", WIDTH), ("ks", WIDTH), ("vs", WIDTH), ("ga", D_MODEL), ("gb", D_MODEL)):
        parts[name] = w[:, o:o + width]
        o += width
    f_pad = jnp.pad(parts["f"], ((0, 0), (0, LANES - N_HEADS)))
    wn = jnp.concatenate([parts["kf"], parts["ks"], parts["ga"], parts["gb"], f_pad], axis=1).astype(BF16)
    wt = jnp.concatenate([parts["qf"] * scale, parts["vf"], parts["qs"] * scale, parts["vs"]], axis=1).T.astype(BF16)
    bfp = jnp.pad(b_forget[0], (0, LANES - N_HEADS)).reshape(1, LANES)

    meta_blk = jnp.concatenate([jnp.zeros((META_PAD, D_MODEL), x.dtype), meta_tokens.astype(x.dtype)], axis=0)
    gains = norm_gains[0].reshape(4, 1, D_MODEL)

    idx = jnp.arange(BLK)
    tri_low = (idx[None, :] <= idx[:, None]).astype(BF16)
    tri_up = (idx[None, :] >= idx[:, None]).astype(BF16)
    tritri = jnp.concatenate([tri_up, tri_up], axis=1)

    kf, ks, ga, gb, qft, vft, qst, vst, ka, ct = _proj_call(x, meta_blk, gains[0], wn, wt, bfp, tri_low, lp)
    o_f = _fox_call(kf, ka, vft, qft, ct, lp)
    o_s = _sb_call(ks, vst, qst, tritri, lp)
    h1 = _post_call(x, meta_blk, o_f, o_s, ga, gb, w_o_fox[0].astype(BF16), w_o_sb[0].astype(BF16),
                    w_out[0].astype(BF16), gains[1], lp)
    return _ffn_call(h1, gains[2], gains[3], w_up[0].astype(BF16), conv_w[0], conv_b[0].reshape(1, 2 * D_FF),
                     w_down[0].astype(BF16), seq)
```

```python
import jax
import jax.numpy as jnp
from jax import lax
from jax.experimental import pallas as pl
from jax.experimental.pallas import tpu as pltpu

D_MODEL = 1024
N_META = 16
HEAD_DIM = 64
N_HEADS = 8
WIDTH = N_HEADS * HEAD_DIM
D_FF = 2816
CONV_WIDTH = 3
EPS = 1e-6

BLK = 256
QBLK = 512
KPQ = QBLK // BLK
PAD = QBLK - N_META
NPRE = QBLK // BLK
META_PAD = BLK - N_META
LANES = 128
HALO = 8
FF_CHUNK = 256
NEG = -1e30
SB_WIN = 3
EXP_UNDERFLOW = -104.0
PAD_BIAS = 32768.0
NORM_SLACK = 1.01
EXIT_SLACK = 2.0
VMEM_LIMIT = 56 * 1024 * 1024

F32 = jnp.float32
BF16 = jnp.bfloat16


def _const_spec(shape):
    zeros = (0,) * len(shape)
    return pl.BlockSpec(shape, lambda *_: zeros, pipeline_mode=pl.Buffered(1))


def _rms_norm(v, g):
    ms = jnp.mean(v * v, axis=-1, keepdims=True)
    return v * lax.rsqrt(ms + EPS) * g


def _split3(v):
    hi = v.astype(BF16)
    r1 = v - hi.astype(F32)
    mid = r1.astype(BF16)
    lo = (r1 - mid.astype(F32)).astype(BF16)
    return hi, mid, lo


def _log_sigmoid(v):
    return jnp.minimum(v, 0.0) - jnp.log1p(jnp.exp(-jnp.abs(v)))


def _tile_input(t, x_ref, meta_ref):
    h = jnp.where(t == NPRE - 1, meta_ref[...], x_ref[0])
    return jnp.where(t < NPRE - 1, 0.0, h)


def _valid_rows(t, shape):
    row = lax.broadcasted_iota(jnp.int32, shape, 0)
    return (t >= NPRE) | ((t == NPRE - 1) & (row >= META_PAD))


def _x_spec():
    return pl.BlockSpec((1, BLK, D_MODEL), lambda bi, t: (bi, jnp.maximum(t - NPRE, 0), 0))


def _proj_kernel(x_ref, meta_ref, g_ref, wn_ref, wt_ref, bf_ref, tri_ref, ind_ref, indt_ref,
                 kf_ref, ks_ref, ga_ref, gb_ref, qft_ref, vft_ref, qst_ref, vst_ref,
                 ka_ref, ct_ref, cf_ref, qn_ref, knmax_ref, cend_ref, carry_ref):
    t = pl.program_id(1)

    @pl.when(t == 0)
    def _():
        carry_ref[...] = jnp.zeros_like(carry_ref)
        knmax_ref[...] = jnp.zeros_like(knmax_ref)

    h = _tile_input(t, x_ref, meta_ref)
    xn = _rms_norm(h, g_ref[...]).astype(BF16)

    pn = jnp.dot(xn, wn_ref[...], preferred_element_type=F32)
    kf_ref[0] = pn[:, 0:WIDTH].astype(BF16)
    ks_ref[0] = pn[:, WIDTH:2 * WIDTH].astype(BF16)
    ga_ref[0] = pn[:, 2 * WIDTH:2 * WIDTH + D_MODEL].astype(BF16)
    gb_ref[0] = pn[:, 2 * WIDTH + D_MODEL:2 * WIDTH + 2 * D_MODEL].astype(BF16)

    pt = lax.dot_general(wt_ref[...], xn, (((1,), (1,)), ((), ())),
                         preferred_element_type=F32)
    qft_ref[0] = pt[0:WIDTH].astype(BF16)
    vft_ref[0] = pt[WIDTH:2 * WIDTH].astype(BF16)
    qst_ref[0] = pt[2 * WIDTH:3 * WIDTH].astype(BF16)
    vst_ref[0] = pt[3 * WIDTH:4 * WIDTH].astype(BF16)

    f = pn[:, 2 * WIDTH + 2 * D_MODEL:] + bf_ref[...]
    lane = lax.broadcasted_iota(jnp.int32, (BLK, LANES), 1)
    valid = (lane < N_HEADS) & _valid_rows(t, (BLK, LANES))
    logf = jnp.where(valid, _log_sigmoid(f), 0.0)
    tri = tri_ref[...]
    c = carry_ref[...]
    for part in _split3(logf):
        c = c + jnp.dot(tri, part, preferred_element_type=F32)
    carry_ref[...] = c[BLK - 1:BLK, :]

    chi, cmid, clo = _split3(c)
    spread = (chi.astype(F32) + pltpu.roll(cmid.astype(F32), N_HEADS, axis=1)
              + pltpu.roll(clo.astype(F32), 2 * N_HEADS, axis=1))
    ones = jnp.where((lane >= 32) & (lane < 32 + 3 * N_HEADS), 1.0, 0.0)
    pad_key = (lane < N_HEADS) & jnp.logical_not(_valid_rows(t, (BLK, LANES)))
    ka_ref[0] = (ones - jnp.where(pad_key, PAD_BIAS, spread)).astype(BF16)
    ct_ref[0] = spread.T[0:32].astype(BF16)

    cf_ref[0] = c.T[0:N_HEADS]
    cend_ref[0, 0] = jnp.broadcast_to(c[BLK - 1:BLK, :], (8, LANES))
    qb = pt[0:WIDTH].astype(BF16).astype(F32)
    qn2 = jnp.dot(indt_ref[...], (qb * qb).astype(BF16), preferred_element_type=F32)
    qn_ref[0] = jnp.sqrt(qn2[0:N_HEADS]) * NORM_SLACK
    kb = pn[:, 0:WIDTH].astype(BF16).astype(F32)
    kn2 = jnp.dot((kb * kb).astype(BF16), ind_ref[...], preferred_element_type=F32)
    kn = jnp.sqrt(jnp.max(kn2, axis=0, keepdims=True)) * NORM_SLACK
    knmax_ref[0] = jnp.maximum(knmax_ref[0], kn)


def _proj_call(x, meta_blk, g0, wn, wt, bfp, tri, ind, indt, lp):
    b = x.shape[0]
    nt = lp // BLK
    row_spec = lambda w: pl.BlockSpec((1, BLK, w), lambda bi, t: (bi, t, 0))
    col_spec = lambda r: pl.BlockSpec((1, r, BLK), lambda bi, t: (bi, 0, t))
    nat = lambda w: jax.ShapeDtypeStruct((b, lp, w), BF16)
    tr = lambda r, dt=BF16: jax.ShapeDtypeStruct((b, r, lp), dt)
    consts = (meta_blk, g0, wn, wt, bfp, tri, ind, indt)
    return pl.pallas_call(
        _proj_kernel,
        grid=(b, nt),
        in_specs=[_x_spec()] + [_const_spec(a.shape) for a in consts],
        out_specs=[row_spec(WIDTH), row_spec(WIDTH), row_spec(D_MODEL), row_spec(D_MODEL),
                   col_spec(WIDTH), col_spec(WIDTH), col_spec(WIDTH), col_spec(WIDTH),
                   row_spec(LANES), col_spec(32), col_spec(N_HEADS), col_spec(N_HEADS),
                   pl.BlockSpec((1, 1, LANES), lambda bi, t: (bi, 0, 0)),
                   pl.BlockSpec((1, 1, 8, LANES), lambda bi, t: (bi, t, 0, 0))],
        out_shape=[nat(WIDTH), nat(WIDTH), nat(D_MODEL), nat(D_MODEL),
                   tr(WIDTH), tr(WIDTH), tr(WIDTH), tr(WIDTH), nat(LANES), tr(32),
                   tr(N_HEADS, F32), tr(N_HEADS, F32),
                   jax.ShapeDtypeStruct((b, 1, LANES), F32),
                   jax.ShapeDtypeStruct((b, nt, 8, LANES), F32)],
        scratch_shapes=[pltpu.VMEM((1, LANES), F32)],
        compiler_params=pltpu.CompilerParams(
            dimension_semantics=("arbitrary", "arbitrary"), vmem_limit_bytes=VMEM_LIMIT),
        name="proj",
    )(x, *consts)


def _tile_mask(kb, qi, strict):
    s_idx = kb * BLK + lax.broadcasted_iota(jnp.int32, (BLK, QBLK), 0)
    t_idx = qi * QBLK + lax.broadcasted_iota(jnp.int32, (BLK, QBLK), 1)
    causal = (s_idx < t_idx) if strict else (s_idx <= t_idx)
    return causal & (s_idx >= PAD)


def _head_rows(qt2, hh):
    z = jnp.zeros((HEAD_DIM, QBLK), BF16)
    q = qt2[hh * HEAD_DIM:(hh + 1) * HEAD_DIM]
    return jnp.concatenate([q, z] if hh == 0 else [z, q], axis=0)


def _fox_kernel(k2_ref, ka_ref, vt2_ref, qt2_ref, ct_ref, cf_ref, qn_ref, knmax_ref, cend_ref,
                o_ref, s_ref, p_ref, acc_ref):
    assert KPQ == 2
    p = pl.program_id(1)
    qi = pl.program_id(2)
    qt2 = qt2_ref[0]
    ct = ct_ref[0]
    r32 = lax.broadcasted_iota(jnp.int32, (32, QBLK), 0)

    rhs = []
    for hh in range(2):
        sel = (r32 < 3 * N_HEADS) & ((r32 % N_HEADS) == 2 * p + hh)
        rhs.append(jnp.concatenate(
            [_head_rows(qt2, hh),
             jnp.where(sel, 1.0, 0.0).astype(BF16),
             jnp.where(sel, ct, jnp.zeros_like(ct)),
             jnp.zeros((64, QBLK), BF16)], axis=0))

    def scores(kb, diagonal, wb):
        off = pl.multiple_of(kb * BLK, BLK)
        lhs = jnp.concatenate([k2_ref[0, pl.ds(off, BLK), :], ka_ref[0, pl.ds(off, BLK), :]], axis=1)
        ok = _tile_mask(kb, qi, strict=False) if diagonal else None
        for hh in range(2):
            s = jnp.dot(lhs, rhs[hh], preferred_element_type=F32)
            s_ref[wb, hh] = jnp.where(ok, s, NEG) if diagonal else s

    def softmax(rb, wb, stats):
        out = []
        for hh in range(2):
            m, l, _ = stats[hh]
            s = s_ref[rb, hh]
            m_new = jnp.maximum(m, jnp.max(s, axis=0, keepdims=True))
            alpha = jnp.exp(m - m_new)
            pr = jnp.exp(s - m_new)
            p_ref[wb, hh] = pr.astype(BF16)
            out.append((m_new, alpha * l + jnp.sum(pr, axis=0, keepdims=True), alpha))
        return tuple(out)

    def values(kb, rb, stats):
        off = pl.multiple_of(kb * BLK, BLK)
        for hh in range(2):
            vt = vt2_ref[0, hh * HEAD_DIM:(hh + 1) * HEAD_DIM, pl.ds(off, BLK)]
            acc_ref[hh] = stats[hh][2] * acc_ref[hh] + jnp.dot(vt, p_ref[rb, hh], preferred_element_type=F32)

    def step(kb_new, rb, wb, state):
        stats, kb_s, kb_p = state
        scores(kb_new, False, wb)
        new = softmax(rb, wb, stats)
        values(kb_p, rb, stats)
        return new, kb_new, kb_s

    acc_ref[...] = jnp.zeros_like(acc_ref)
    one = (jnp.full((1, QBLK), NEG, F32), jnp.zeros((1, QBLK), F32), jnp.ones((1, QBLK), F32))
    top = KPQ * qi + 1

    scores(top, True, 0)
    scores(top - 1, True, 1)
    stats = softmax(0, 1, (one, one))
    state = step(jnp.maximum(top - 2, 0), 1, 0, (stats, top - 1, top))

    lane = lax.broadcasted_iota(jnp.int32, (1, LANES), 1)
    need = jnp.float32(0.0)
    for hh in range(2):
        head = 2 * p + hh
        m = state[0][hh][0]
        kn = jnp.max(jnp.where(lane == head, knmax_ref[0], 0.0), axis=1, keepdims=True)
        bound = qn_ref[0, pl.ds(head, 1), :] * kn + cf_ref[0, pl.ds(head, 1), :] - m
        limit = jnp.max(bound, axis=1, keepdims=True) - EXP_UNDERFLOW + EXIT_SLACK
        live = (cend_ref[0, pl.ds(head, 1), :] <= limit) & (lane >= 1) & (lane <= top - 3)
        need = jnp.maximum(need, jnp.sum(jnp.where(live, 1.0, 0.0)))
    need = need.astype(jnp.int32)

    def two_steps(i, st):
        st = step(jnp.maximum(top - 3 - 2 * i, 0), 0, 1, st)
        return step(jnp.maximum(top - 4 - 2 * i, 0), 1, 0, st)

    stats, kb_s, kb_p = lax.fori_loop(0, (need + 1) // 2, two_steps, state)
    last = softmax(0, 1, stats)
    values(kb_p, 0, stats)
    values(kb_s, 1, last)
    rows = [acc_ref[hh] / last[hh][1] for hh in range(2)]
    o_ref[0] = jnp.concatenate(rows, axis=0).T.astype(BF16)


def _sb_kernel(k2_ref, vt2_ref, qt2_ref, tt_ref, o_ref):
    assert KPQ == 2 and SB_WIN == 3 and PAD // BLK == 1
    qi = pl.program_id(2)
    qt2 = qt2_ref[0]
    rhs = [_head_rows(qt2, hh) for hh in range(2)]
    tt = tt_ref[...]

    def log_keep(z):
        return jnp.minimum(-z, 0.0) - jnp.log(1.0 + jnp.exp(-jnp.abs(z)))

    def suffix_sum(lk):
        hi = lk.astype(BF16)
        lo = (lk - hi.astype(F32)).astype(BF16)
        return jnp.dot(tt, jnp.concatenate([hi, lo], axis=0), preferred_element_type=F32)

    def block(z, ok, cr):
        lk = log_keep(z)
        if ok is not None:
            lk = jnp.where(ok, lk, 0.0)
        cum = suffix_sum(lk)
        a = jnp.exp(z + cum + cr)
        if ok is not None:
            a = jnp.where(ok, a, 0.0)
        return a.astype(BF16), cr + cum[0:1, :]

    top = KPQ * qi + 1
    half_ok = _tile_mask(top, qi, strict=True)[:, BLK:]
    zeros_a = jnp.zeros((BLK, BLK), BF16)
    zeros_c = jnp.zeros((1, BLK), F32)

    def upper_half(hh):
        off = pl.multiple_of(top * BLK, BLK)
        z = jnp.dot(k2_ref[0, pl.ds(off, BLK), :], rhs[hh][:, BLK:], preferred_element_type=F32)
        a, cr = block(z, half_ok, zeros_c)
        return jnp.concatenate([zeros_a, a], axis=1), jnp.concatenate([zeros_c, cr], axis=1)

    def first_block(_):
        out = []
        for hh in range(2):
            a, cr = upper_half(hh)
            off = pl.multiple_of(top * BLK, BLK)
            vt = vt2_ref[0, hh * HEAD_DIM:(hh + 1) * HEAD_DIM, pl.ds(off, BLK)]
            out.append((cr, jnp.dot(vt, a, preferred_element_type=F32)))
        return tuple(out)

    def window(pad_keys):
        def run(_):
            off0 = pl.multiple_of((top - 2) * BLK, BLK)
            kwin = k2_ref[0, pl.ds(off0, 2 * BLK), :]
            ok1 = _tile_mask(top - 1, qi, strict=True)
            ok0 = _tile_mask(top - 2, qi, strict=True) if pad_keys else None
            out = []
            for hh in range(2):
                z = jnp.dot(kwin, rhs[hh], preferred_element_type=F32)
                a2, cr = upper_half(hh)
                a1, cr = block(z[BLK:], ok1, cr)
                a0, cr = block(z[:BLK], ok0, cr)
                vt = vt2_ref[0, hh * HEAD_DIM:(hh + 1) * HEAD_DIM, pl.ds(off0, SB_WIN * BLK)]
                acc = jnp.dot(vt, jnp.concatenate([a0, a1, a2], axis=0), preferred_element_type=F32)
                out.append((cr, acc))
            return tuple(out)
        return run

    state = lax.switch(jnp.minimum(qi, 2), [first_block, window(True), window(False)], 0)

    def tile(kb, masked, st):
        off = pl.multiple_of(kb * BLK, BLK)
        k2 = k2_ref[0, pl.ds(off, BLK), :]
        ok = _tile_mask(kb, qi, strict=True) if masked else None
        out = []
        for hh in range(2):
            cr, acc = st[hh]
            a, cr = block(jnp.dot(k2, rhs[hh], preferred_element_type=F32), ok, cr)
            vt = vt2_ref[0, hh * HEAD_DIM:(hh + 1) * HEAD_DIM, pl.ds(off, BLK)]
            out.append((cr, acc + jnp.dot(vt, a, preferred_element_type=F32)))
        return tuple(out)

    def live(st):
        return jnp.maximum(jnp.max(st[0][0]), jnp.max(st[1][0])) >= EXP_UNDERFLOW

    kb, state = lax.while_loop(lambda c: (c[0] >= 2) & live(c[1]),
                               lambda c: (c[0] - 1, tile(c[0], False, c[1])),
                               (top - SB_WIN, state))
    state = lax.cond((kb == 1) & live(state), lambda st: tile(1, True, st), lambda st: st, state)
    o_ref[0] = jnp.concatenate([state[0][1], state[1][1]], axis=0).T.astype(BF16)


def _attn_call(kernel, name, args, in_specs, scratch, b, lp):
    return pl.pallas_call(
        kernel,
        grid=(b, N_HEADS // 2, lp // QBLK),
        in_specs=in_specs,
        out_specs=pl.BlockSpec((1, QBLK, LANES), lambda bi, p, qi: (bi, qi, p)),
        out_shape=jax.ShapeDtypeStruct((b, lp, WIDTH), BF16),
        scratch_shapes=scratch,
        compiler_params=pltpu.CompilerParams(
            dimension_semantics=("arbitrary", "arbitrary", "arbitrary"), vmem_limit_bytes=VMEM_LIMIT),
        name=name,
    )(*args)


def _pair_specs(lp):
    k_spec = pl.BlockSpec((1, lp, LANES), lambda bi, p, qi: (bi, 0, p))
    vt_spec = pl.BlockSpec((1, LANES, lp), lambda bi, p, qi: (bi, p, 0))
    qt_spec = pl.BlockSpec((1, LANES, QBLK), lambda bi, p, qi: (bi, p, qi))
    return k_spec, vt_spec, qt_spec


def _fox_call(kf, ka, vft, qft, ct, cf, qn, knmax, cend, lp):
    k_spec, vt_spec, qt_spec = _pair_specs(lp)
    ka_spec = pl.BlockSpec((1, lp, LANES), lambda bi, p, qi: (bi, 0, 0))
    ct_spec = pl.BlockSpec((1, 32, QBLK), lambda bi, p, qi: (bi, 0, qi))
    head_spec = pl.BlockSpec((1, N_HEADS, QBLK), lambda bi, p, qi: (bi, 0, qi))
    knmax_spec = pl.BlockSpec((1, 1, LANES), lambda bi, p, qi: (bi, 0, 0))
    cend_spec = pl.BlockSpec((1, N_HEADS, LANES), lambda bi, p, qi: (bi, 0, 0))
    scratch = [pltpu.VMEM((2, 2, BLK, QBLK), F32),
               pltpu.VMEM((2, 2, BLK, QBLK), BF16),
               pltpu.VMEM((2, HEAD_DIM, QBLK), F32)]
    return _attn_call(_fox_kernel, "fox", (kf, ka, vft, qft, ct, cf, qn, knmax, cend),
                      [k_spec, ka_spec, vt_spec, qt_spec, ct_spec, head_spec, head_spec, knmax_spec, cend_spec],
                      scratch, kf.shape[0], lp)


def _sb_call(ks, vst, qst, tritri, lp):
    k_spec, vt_spec, qt_spec = _pair_specs(lp)
    return _attn_call(_sb_kernel, "sb", (ks, vst, qst, tritri),
                      [k_spec, vt_spec, qt_spec, _const_spec(tritri.shape)], [], ks.shape[0], lp)


def _post_kernel(x_ref, meta_ref, of_ref, os_ref, ga_ref, gb_ref, wof_ref, wos_ref, wout_ref, g_ref,
                 h1_ref):
    t = pl.program_id(1)
    h = _tile_input(t, x_ref, meta_ref)
    ya = jnp.dot(of_ref[0], wof_ref[...], preferred_element_type=F32)
    yb = jnp.dot(os_ref[0], wos_ref[...], preferred_element_type=F32)
    merged = (jax.nn.sigmoid(ga_ref[0].astype(F32)) * ya + jax.nn.sigmoid(gb_ref[0].astype(F32)) * yb)
    mixed = jnp.dot(merged.astype(BF16), wout_ref[...], preferred_element_type=F32)
    h1 = h + _rms_norm(mixed, g_ref[...])
    h1_ref[0] = jnp.where(_valid_rows(t, (BLK, 1)), h1, 0.0)


def _post_call(x, meta_blk, o_f, o_s, ga, gb, wof, wos, wout, g1, lp):
    b = x.shape[0]
    nt = lp // BLK
    row_spec = lambda w: pl.BlockSpec((1, BLK, w), lambda bi, t: (bi, t, 0))
    return pl.pallas_call(
        _post_kernel,
        grid=(b, nt),
        in_specs=[_x_spec(), _const_spec(meta_blk.shape), row_spec(WIDTH), row_spec(WIDTH),
                  row_spec(D_MODEL), row_spec(D_MODEL), _const_spec(wof.shape), _const_spec(wos.shape),
                  _const_spec(wout.shape), _const_spec(g1.shape)],
        out_specs=row_spec(D_MODEL),
        out_shape=jax.ShapeDtypeStruct((b, lp, D_MODEL), F32),
        compiler_params=pltpu.CompilerParams(
            dimension_semantics=("arbitrary", "arbitrary"), vmem_limit_bytes=VMEM_LIMIT),
        name="post",
    )(x, meta_blk, o_f, o_s, ga, gb, wof, wos, wout, g1)


def _ffn_kernel(h1_ref, g2_ref, g3_ref, wup_ref, cw_ref, cb_ref, wdn_ref, out_ref, u_ref):
    t = pl.program_id(1)

    @pl.when(t == 0)
    def _():
        u_ref[0:HALO, :] = jnp.zeros((HALO, 2 * D_FF), F32)

    @pl.when(t > 0)
    def _():
        u_ref[0:HALO, :] = u_ref[BLK:BLK + HALO, :]

    h1 = h1_ref[0]
    xn = _rms_norm(h1, g2_ref[...]).astype(BF16)
    u_ref[HALO:HALO + BLK, :] = jnp.dot(xn, wup_ref[...], preferred_element_type=F32)

    def conv(cols):
        out = cb_ref[:, cols]
        for i in range(CONV_WIDTH):
            lo = HALO - (CONV_WIDTH - 1) + i
            out = out + cw_ref[i:i + 1, cols] * u_ref[lo:lo + BLK, cols]
        return out

    ffn = jnp.zeros((BLK, D_MODEL), F32)
    for j in range(D_FF // FF_CHUNK):
        gate = conv(slice(j * FF_CHUNK, (j + 1) * FF_CHUNK))
        val = conv(slice(D_FF + j * FF_CHUNK, D_FF + (j + 1) * FF_CHUNK))
        act = (jax.nn.gelu(gate, approximate=True) * val).astype(BF16)
        ffn = ffn + jnp.dot(act, wdn_ref[j * FF_CHUNK:(j + 1) * FF_CHUNK, :], preferred_element_type=F32)

    out_ref[0] = h1 + _rms_norm(ffn, g3_ref[...])


def _ffn_call(h1, g2, g3, wup, cw, cb, wdn, seq):
    b, lp, _ = h1.shape
    nt = lp // BLK
    return pl.pallas_call(
        _ffn_kernel,
        grid=(b, nt),
        in_specs=[pl.BlockSpec((1, BLK, D_MODEL), lambda bi, t: (bi, t, 0)),
                  _const_spec(g2.shape), _const_spec(g3.shape), _const_spec(wup.shape),
                  _const_spec(cw.shape), _const_spec(cb.shape), _const_spec(wdn.shape)],
        out_specs=pl.BlockSpec((1, BLK, D_MODEL), lambda bi, t: (bi, jnp.maximum(t - NPRE, 0), 0)),
        out_shape=jax.ShapeDtypeStruct((b, seq, D_MODEL), F32),
        scratch_shapes=[pltpu.VMEM((HALO + BLK, 2 * D_FF), F32)],
        compiler_params=pltpu.CompilerParams(
            dimension_semantics=("arbitrary", "arbitrary"), vmem_limit_bytes=VMEM_LIMIT),
        name="ffn",
    )(h1, g2, g3, wup, cw, cb, wdn)


def kernel(x, meta_tokens, norm_gains, w_in, b_forget, w_o_fox, w_o_sb, w_out, w_up, conv_w, conv_b, w_down):
    b, seq, d = x.shape
    assert d == D_MODEL and seq % QBLK == 0 and D_FF % FF_CHUNK == 0 and (QBLK + seq) // BLK <= LANES
    assert norm_gains.shape[0] == 1, "single-layer block"
    lp = QBLK + seq
    scale = HEAD_DIM ** -0.5

    perm = jnp.argsort(b_forget[0])
    by_head = lambda a: a.reshape(D_MODEL, N_HEADS, -1)[:, perm].reshape(D_MODEL, -1)

    w = w_in[0]
    o = 0
    parts = {}
    for name, width in (("qf", WIDTH), ("kf", WIDTH), ("vf", WIDTH), ("f", N_HEADS),
                        ("qs", WIDTH), ("ks", WIDTH), ("vs", WIDTH), ("ga", D_MODEL), ("gb", D_MODEL)):
        parts[name] = w[:, o:o + width]
        o += width
    for name in ("qf", "kf", "vf", "f"):
        parts[name] = by_head(parts[name])
    f_pad = jnp.pad(parts["f"], ((0, 0), (0, LANES - N_HEADS)))
    wn = jnp.concatenate([parts["kf"], parts["ks"], parts["ga"], parts["gb"], f_pad], axis=1).astype(BF16)
    wt = jnp.concatenate([parts["qf"] * scale, parts["vf"], parts["qs"] * scale, parts["vs"]], axis=1).T.astype(BF16)
    bfp = jnp.pad(b_forget[0][perm], (0, LANES - N_HEADS)).reshape(1, LANES)
    wof = w_o_fox[0].reshape(N_HEADS, HEAD_DIM, D_MODEL)[perm].reshape(WIDTH, D_MODEL).astype(BF16)

    meta_blk = jnp.concatenate([jnp.zeros((META_PAD, D_MODEL), x.dtype), meta_tokens.astype(x.dtype)], axis=0)
    gains = norm_gains[0].reshape(4, 1, D_MODEL)

    idx = jnp.arange(BLK)
    tri_low = (idx[None, :] <= idx[:, None]).astype(BF16)
    tri_up = (idx[None, :] >= idx[:, None]).astype(BF16)
    tritri = jnp.concatenate([tri_up, tri_up], axis=1)
    ind = (jnp.arange(WIDTH)[:, None] // HEAD_DIM == jnp.arange(LANES)[None, :]).astype(BF16)
    indt = ind[:, :16].T

    (kf, ks, ga, gb, qft, vft, qst, vst, ka, ct, cf, qn, knmax, cend) = _proj_call(
        x, meta_blk, gains[0], wn, wt, bfp, tri_low, ind, indt, lp)
    cend = jnp.pad(cend[:, :, 0, :N_HEADS].transpose(0, 2, 1), ((0, 0), (0, 0), (0, LANES - lp // BLK)))
    o_f = _fox_call(kf, ka, vft, qft, ct, cf, qn, knmax, cend, lp)
    o_s = _sb_call(ks, vst, qst, tritri, lp)
    h1 = _post_call(x, meta_blk, o_f, o_s, ga, gb, wof, w_o_sb[0].astype(BF16),
                    w_out[0].astype(BF16), gains[1], lp)
    return _ffn_call(h1, gains[2], gains[3], w_up[0].astype(BF16), conv_w[0], conv_b[0].reshape(1, 2 * D_FF),
                     w_down[0].astype(BF16), seq)
```

```python
import jax
import jax.numpy as jnp
from jax import lax
from jax.experimental import pallas as pl
from jax.experimental.pallas import tpu as pltpu

D_MODEL = 1024
N_META = 16
HEAD_DIM = 64
N_HEADS = 8
WIDTH = N_HEADS * HEAD_DIM
D_FF = 2816
CONV_WIDTH = 3
EPS = 1e-6

BLK = 256
QBLK = 512
KPQ = QBLK // BLK
PAD = QBLK - N_META
NPRE = QBLK // BLK
META_PAD = BLK - N_META
LANES = 128
HALO = 8
FF_CHUNK = 256
NEG = -1e30
SB_WIN = 3
EXP_UNDERFLOW = -104.0
PAD_BIAS = 32768.0
NORM_SLACK = 1.01
EXIT_SLACK = 2.0
SOFTMAX_ROWS = 64
DENOM_ROWS = 16
VMEM_LIMIT = 56 * 1024 * 1024

F32 = jnp.float32
BF16 = jnp.bfloat16


def _const_spec(shape):
    zeros = (0,) * len(shape)
    return pl.BlockSpec(shape, lambda *_: zeros, pipeline_mode=pl.Buffered(1))


def _rms_norm(v, g):
    ms = jnp.mean(v * v, axis=-1, keepdims=True)
    return v * lax.rsqrt(ms + EPS) * g


def _split3(v):
    hi = v.astype(BF16)
    r1 = v - hi.astype(F32)
    mid = r1.astype(BF16)
    lo = (r1 - mid.astype(F32)).astype(BF16)
    return hi, mid, lo


def _log_sigmoid(v):
    return jnp.minimum(v, 0.0) - jnp.log1p(jnp.exp(-jnp.abs(v)))


def _tile_input(t, x_ref, meta_ref):
    h = jnp.where(t == NPRE - 1, meta_ref[...], x_ref[0])
    return jnp.where(t < NPRE - 1, 0.0, h)


def _valid_rows(t, shape):
    row = lax.broadcasted_iota(jnp.int32, shape, 0)
    return (t >= NPRE) | ((t == NPRE - 1) & (row >= META_PAD))


def _x_spec():
    return pl.BlockSpec((1, BLK, D_MODEL), lambda bi, t: (bi, jnp.maximum(t - NPRE, 0), 0))


def _proj_kernel(x_ref, meta_ref, g_ref, wn_ref, wt_ref, bf_ref, tri_ref, ind_ref, indt_ref,
                 kf_ref, ks_ref, ga_ref, gb_ref, qft_ref, vft_ref, qst_ref, vst_ref,
                 ka_ref, ct_ref, cf_ref, qn_ref, knmax_ref, cend_ref, carry_ref):
    t = pl.program_id(1)

    @pl.when(t == 0)
    def _():
        carry_ref[...] = jnp.zeros_like(carry_ref)
        knmax_ref[...] = jnp.zeros_like(knmax_ref)

    h = _tile_input(t, x_ref, meta_ref)
    xn = _rms_norm(h, g_ref[...]).astype(BF16)

    pn = jnp.dot(xn, wn_ref[...], preferred_element_type=F32)
    kf_ref[0] = pn[:, 0:WIDTH].astype(BF16)
    ks_ref[0] = pn[:, WIDTH:2 * WIDTH].astype(BF16)
    ga_ref[0] = pn[:, 2 * WIDTH:2 * WIDTH + D_MODEL].astype(BF16)
    gb_ref[0] = pn[:, 2 * WIDTH + D_MODEL:2 * WIDTH + 2 * D_MODEL].astype(BF16)

    pt = lax.dot_general(wt_ref[...], xn, (((1,), (1,)), ((), ())),
                         preferred_element_type=F32)
    qft_ref[0] = pt[0:WIDTH].astype(BF16)
    vft_ref[0] = pt[WIDTH:2 * WIDTH].astype(BF16)
    qst_ref[0] = pt[2 * WIDTH:3 * WIDTH].astype(BF16)
    vst_ref[0] = pt[3 * WIDTH:4 * WIDTH].astype(BF16)

    f = pn[:, 2 * WIDTH + 2 * D_MODEL:] + bf_ref[...]
    lane = lax.broadcasted_iota(jnp.int32, (BLK, LANES), 1)
    valid = (lane < N_HEADS) & _valid_rows(t, (BLK, LANES))
    logf = jnp.where(valid, _log_sigmoid(f), 0.0)
    tri = tri_ref[...]
    c = carry_ref[...]
    for part in _split3(logf):
        c = c + jnp.dot(tri, part, preferred_element_type=F32)
    carry_ref[...] = c[BLK - 1:BLK, :]

    chi, cmid, clo = _split3(c)
    spread = (chi.astype(F32) + pltpu.roll(cmid.astype(F32), N_HEADS, axis=1)
              + pltpu.roll(clo.astype(F32), 2 * N_HEADS, axis=1))
    ones = jnp.where((lane >= 32) & (lane < 32 + 3 * N_HEADS), 1.0, 0.0)
    pad_key = (lane < N_HEADS) & jnp.logical_not(_valid_rows(t, (BLK, LANES)))
    ka_ref[0] = (ones - jnp.where(pad_key, PAD_BIAS, spread)).astype(BF16)
    ct_ref[0] = spread.T[0:32].astype(BF16)

    cf_ref[0] = c.T[0:N_HEADS]
    cend_ref[0, 0] = jnp.broadcast_to(c[BLK - 1:BLK, :], (8, LANES))
    qb = pt[0:WIDTH].astype(BF16).astype(F32)
    qn2 = jnp.dot(indt_ref[...], (qb * qb).astype(BF16), preferred_element_type=F32)
    qn_ref[0] = jnp.sqrt(qn2[0:N_HEADS]) * NORM_SLACK
    kb = pn[:, 0:WIDTH].astype(BF16).astype(F32)
    kn2 = jnp.dot((kb * kb).astype(BF16), ind_ref[...], preferred_element_type=F32)
    kn = jnp.sqrt(jnp.max(kn2, axis=0, keepdims=True)) * NORM_SLACK
    knmax_ref[0] = jnp.maximum(knmax_ref[0], kn)


def _proj_call(x, meta_blk, g0, wn, wt, bfp, tri, ind, indt, lp):
    b = x.shape[0]
    nt = lp // BLK
    row_spec = lambda w: pl.BlockSpec((1, BLK, w), lambda bi, t: (bi, t, 0))
    col_spec = lambda r: pl.BlockSpec((1, r, BLK), lambda bi, t: (bi, 0, t))
    nat = lambda w: jax.ShapeDtypeStruct((b, lp, w), BF16)
    tr = lambda r, dt=BF16: jax.ShapeDtypeStruct((b, r, lp), dt)
    consts = (meta_blk, g0, wn, wt, bfp, tri, ind, indt)
    return pl.pallas_call(
        _proj_kernel,
        grid=(b, nt),
        in_specs=[_x_spec()] + [_const_spec(a.shape) for a in consts],
        out_specs=[row_spec(WIDTH), row_spec(WIDTH), row_spec(D_MODEL), row_spec(D_MODEL),
                   col_spec(WIDTH), col_spec(WIDTH), col_spec(WIDTH), col_spec(WIDTH),
                   row_spec(LANES), col_spec(32), col_spec(N_HEADS), col_spec(N_HEADS),
                   pl.BlockSpec((1, 1, LANES), lambda bi, t: (bi, 0, 0)),
                   pl.BlockSpec((1, 1, 8, LANES), lambda bi, t: (bi, t, 0, 0))],
        out_shape=[nat(WIDTH), nat(WIDTH), nat(D_MODEL), nat(D_MODEL),
                   tr(WIDTH), tr(WIDTH), tr(WIDTH), tr(WIDTH), nat(LANES), tr(32),
                   tr(N_HEADS, F32), tr(N_HEADS, F32),
                   jax.ShapeDtypeStruct((b, 1, LANES), F32),
                   jax.ShapeDtypeStruct((b, nt, 8, LANES), F32)],
        scratch_shapes=[pltpu.VMEM((1, LANES), F32)],
        compiler_params=pltpu.CompilerParams(
            dimension_semantics=("arbitrary", "arbitrary"), vmem_limit_bytes=VMEM_LIMIT),
        name="proj",
    )(x, *consts)


def _tile_mask(kb, qi, strict):
    s_idx = kb * BLK + lax.broadcasted_iota(jnp.int32, (BLK, QBLK), 0)
    t_idx = qi * QBLK + lax.broadcasted_iota(jnp.int32, (BLK, QBLK), 1)
    causal = (s_idx < t_idx) if strict else (s_idx <= t_idx)
    return causal & (s_idx >= PAD)


def _head_rows(qt2, hh):
    z = jnp.zeros((HEAD_DIM, QBLK), BF16)
    q = qt2[hh * HEAD_DIM:(hh + 1) * HEAD_DIM]
    return jnp.concatenate([q, z] if hh == 0 else [z, q], axis=0)


def _fox_kernel(k2_ref, ka_ref, vt2_ref, qt2_ref, ct_ref, cf_ref, qn_ref, knmax_ref, cend_ref,
                o_ref, s_ref, p_ref, acc_ref):
    p = pl.program_id(1)
    qi = pl.program_id(2)
    qt2 = qt2_ref[0]
    ct = ct_ref[0]
    r32 = lax.broadcasted_iota(jnp.int32, (32, QBLK), 0)

    rhs = []
    for hh in range(2):
        sel = (r32 < 3 * N_HEADS) & ((r32 % N_HEADS) == 2 * p + hh)
        rhs.append(jnp.concatenate(
            [_head_rows(qt2, hh),
             jnp.where(sel, 1.0, 0.0).astype(BF16),
             jnp.where(sel, ct, jnp.zeros_like(ct)),
             jnp.zeros((64, QBLK), BF16)], axis=0))

    def scores(kb, diagonal, w):
        off = pl.multiple_of(kb * QBLK, QBLK)
        lhs = jnp.concatenate([k2_ref[0, pl.ds(off, QBLK), :], ka_ref[0, pl.ds(off, QBLK), :]], axis=1)
        if diagonal:
            ok = (lax.broadcasted_iota(jnp.int32, (QBLK, QBLK), 0)
                  <= lax.broadcasted_iota(jnp.int32, (QBLK, QBLK), 1))
        for hh in range(2):
            s = jnp.dot(lhs, rhs[hh], preferred_element_type=F32)
            s_ref[w, hh] = jnp.where(ok, s, NEG) if diagonal else s

    def softmax(b, stats):
        out = []
        for hh in range(2):
            m = stats[hh][0]
            m_new = jnp.maximum(m, jnp.max(s_ref[b, hh], axis=0, keepdims=True))
            for r in range(0, QBLK, SOFTMAX_ROWS):
                pr = jnp.exp(s_ref[b, hh, r:r + SOFTMAX_ROWS, :] - m_new)
                p_ref[b, hh, r:r + SOFTMAX_ROWS, :] = pr.astype(BF16)
            out.append((m_new, jnp.exp(m - m_new)))
        return tuple(out)

    def values(kb, b, stats):
        off = pl.multiple_of(kb * QBLK, QBLK)
        for hh in range(2):
            vt = vt2_ref[0, hh * HEAD_DIM:(hh + 1) * HEAD_DIM, pl.ds(off, QBLK)]
            lhs = jnp.concatenate([vt, jnp.ones((DENOM_ROWS, QBLK), BF16)], axis=0)
            acc_ref[hh] = stats[hh][1] * acc_ref[hh] + jnp.dot(lhs, p_ref[b, hh], preferred_element_type=F32)

    acc_ref[...] = jnp.zeros_like(acc_ref)
    one = (jnp.full((1, QBLK), NEG, F32), jnp.ones((1, QBLK), F32))

    def single_block(_):
        scores(0, True, 0)
        stats = softmax(0, (one, one))
        values(0, 0, stats)
        return stats

    def sweep(_):
        scores(qi, True, 0)
        scores(qi - 1, False, 1)
        stats = softmax(0, (one, one))

        lane = lax.broadcasted_iota(jnp.int32, (1, LANES), 1)
        need = jnp.float32(0.0)
        for hh in range(2):
            head = 2 * p + hh
            kn = jnp.max(jnp.where(lane == head, knmax_ref[0], 0.0), axis=1, keepdims=True)
            bound = qn_ref[0, pl.ds(head, 1), :] * kn + cf_ref[0, pl.ds(head, 1), :] - stats[hh][0]
            limit = jnp.max(bound, axis=1, keepdims=True) - EXP_UNDERFLOW + EXIT_SLACK
            live = (cend_ref[0, pl.ds(head, 1), :] <= limit) & (lane <= qi - 2)
            need = jnp.maximum(need, jnp.sum(jnp.where(live, 1.0, 0.0)))
        n_blocks = 2 + need.astype(jnp.int32)

        def body(k, cur, st):
            scores(qi - k, False, cur)
            new = softmax(1 - cur, st)
            values(qi - k + 2, cur, st)
            return new

        def drain(cur, st):
            last = softmax(1 - cur, st)
            values(qi - n_blocks + 2, cur, st)
            values(qi - n_blocks + 1, 1 - cur, last)
            return last

        def by_parity(k, fn, st):
            return lax.cond(lax.bitwise_and(k, 1) == 0, lambda s: fn(0, s), lambda s: fn(1, s), st)

        stats = lax.fori_loop(2, n_blocks, lambda k, st: by_parity(k, lambda c, s: body(k, c, s), st), stats)
        return by_parity(n_blocks, drain, stats)

    lax.cond(qi == 0, single_block, sweep, 0)
    rows = [acc_ref[hh, 0:HEAD_DIM, :] / acc_ref[hh, HEAD_DIM:HEAD_DIM + 1, :] for hh in range(2)]
    o_ref[0] = jnp.concatenate(rows, axis=0).T.astype(BF16)


def _sb_kernel(k2_ref, vt2_ref, qt2_ref, tt_ref, o_ref):
    assert KPQ == 2 and SB_WIN == 3 and PAD // BLK == 1
    qi = pl.program_id(2)
    qt2 = qt2_ref[0]
    rhs = [_head_rows(qt2, hh) for hh in range(2)]
    tt = tt_ref[...]

    def log_keep(z):
        return jnp.minimum(-z, 0.0) - jnp.log(1.0 + jnp.exp(-jnp.abs(z)))

    def suffix_sum(lk):
        return jnp.dot(tt, lk.astype(BF16), preferred_element_type=F32)

    def block(z, ok, cr):
        lk = log_keep(z)
        if ok is not None:
            lk = jnp.where(ok, lk, 0.0)
        cum = suffix_sum(lk)
        a = jnp.exp(z + cum + cr)
        if ok is not None:
            a = jnp.where(ok, a, 0.0)
        return a.astype(BF16), cr + cum[0:1, :]

    top = KPQ * qi + 1
    half_ok = _tile_mask(top, qi, strict=True)[:, BLK:]
    zeros_a = jnp.zeros((BLK, BLK), BF16)
    zeros_c = jnp.zeros((1, BLK), F32)

    def upper_half(hh):
        off = pl.multiple_of(top * BLK, BLK)
        z = jnp.dot(k2_ref[0, pl.ds(off, BLK), :], rhs[hh][:, BLK:], preferred_element_type=F32)
        a, cr = block(z, half_ok, zeros_c)
        return jnp.concatenate([zeros_a, a], axis=1), jnp.concatenate([zeros_c, cr], axis=1)

    def first_block(_):
        out = []
        for hh in range(2):
            a, cr = upper_half(hh)
            off = pl.multiple_of(top * BLK, BLK)
            vt = vt2_ref[0, hh * HEAD_DIM:(hh + 1) * HEAD_DIM, pl.ds(off, BLK)]
            out.append((cr, jnp.dot(vt, a, preferred_element_type=F32)))
        return tuple(out)

    def window(pad_keys):
        def run(_):
            off0 = pl.multiple_of((top - 2) * BLK, BLK)
            kwin = k2_ref[0, pl.ds(off0, 2 * BLK), :]
            ok1 = _tile_mask(top - 1, qi, strict=True)
            ok0 = _tile_mask(top - 2, qi, strict=True) if pad_keys else None
            out = []
            for hh in range(2):
                z = jnp.dot(kwin, rhs[hh], preferred_element_type=F32)
                a2, cr = upper_half(hh)
                a1, cr = block(z[BLK:], ok1, cr)
                a0, cr = block(z[:BLK], ok0, cr)
                vt = vt2_ref[0, hh * HEAD_DIM:(hh + 1) * HEAD_DIM, pl.ds(off0, SB_WIN * BLK)]
                acc = jnp.dot(vt, jnp.concatenate([a0, a1, a2], axis=0), preferred_element_type=F32)
                out.append((cr, acc))
            return tuple(out)
        return run

    state = lax.switch(jnp.minimum(qi, 2), [first_block, window(True), window(False)], 0)

    def tile(kb, masked, st):
        off = pl.multiple_of(kb * BLK, BLK)
        k2 = k2_ref[0, pl.ds(off, BLK), :]
        ok = _tile_mask(kb, qi, strict=True) if masked else None
        out = []
        for hh in range(2):
            cr, acc = st[hh]
            a, cr = block(jnp.dot(k2, rhs[hh], preferred_element_type=F32), ok, cr)
            vt = vt2_ref[0, hh * HEAD_DIM:(hh + 1) * HEAD_DIM, pl.ds(off, BLK)]
            out.append((cr, acc + jnp.dot(vt, a, preferred_element_type=F32)))
        return tuple(out)

    def live(st):
        return jnp.maximum(jnp.max(st[0][0]), jnp.max(st[1][0])) >= EXP_UNDERFLOW

    kb, state = lax.while_loop(lambda c: (c[0] >= 2) & live(c[1]),
                               lambda c: (c[0] - 1, tile(c[0], False, c[1])),
                               (top - SB_WIN, state))
    state = lax.cond((kb == 1) & live(state), lambda st: tile(1, True, st), lambda st: st, state)
    o_ref[0] = jnp.concatenate([state[0][1], state[1][1]], axis=0).T.astype(BF16)


def _attn_call(kernel, name, args, in_specs, scratch, b, lp):
    return pl.pallas_call(
        kernel,
        grid=(b, N_HEADS // 2, lp // QBLK),
        in_specs=in_specs,
        out_specs=pl.BlockSpec((1, QBLK, LANES), lambda bi, p, qi: (bi, qi, p)),
        out_shape=jax.ShapeDtypeStruct((b, lp, WIDTH), BF16),
        scratch_shapes=scratch,
        compiler_params=pltpu.CompilerParams(
            dimension_semantics=("arbitrary", "arbitrary", "arbitrary"), vmem_limit_bytes=VMEM_LIMIT),
        name=name,
    )(*args)


def _pair_specs(lp):
    k_spec = pl.BlockSpec((1, lp, LANES), lambda bi, p, qi: (bi, 0, p))
    vt_spec = pl.BlockSpec((1, LANES, lp), lambda bi, p, qi: (bi, p, 0))
    qt_spec = pl.BlockSpec((1, LANES, QBLK), lambda bi, p, qi: (bi, p, qi))
    return k_spec, vt_spec, qt_spec


def _fox_call(kf, ka, vft, qft, ct, cf, qn, knmax, cend, lp):
    k_spec, vt_spec, qt_spec = _pair_specs(lp)
    ka_spec = pl.BlockSpec((1, lp, LANES), lambda bi, p, qi: (bi, 0, 0))
    ct_spec = pl.BlockSpec((1, 32, QBLK), lambda bi, p, qi: (bi, 0, qi))
    head_spec = pl.BlockSpec((1, N_HEADS, QBLK), lambda bi, p, qi: (bi, 0, qi))
    knmax_spec = pl.BlockSpec((1, 1, LANES), lambda bi, p, qi: (bi, 0, 0))
    cend_spec = pl.BlockSpec((1, N_HEADS, LANES), lambda bi, p, qi: (bi, 0, 0))
    scratch = [pltpu.VMEM((2, 2, QBLK, QBLK), F32),
               pltpu.VMEM((2, 2, QBLK, QBLK), BF16),
               pltpu.VMEM((2, HEAD_DIM + DENOM_ROWS, QBLK), F32)]
    return _attn_call(_fox_kernel, "fox", (kf, ka, vft, qft, ct, cf, qn, knmax, cend),
                      [k_spec, ka_spec, vt_spec, qt_spec, ct_spec, head_spec, head_spec, knmax_spec, cend_spec],
                      scratch, kf.shape[0], lp)


def _sb_call(ks, vst, qst, tri_up, lp):
    k_spec, vt_spec, qt_spec = _pair_specs(lp)
    return _attn_call(_sb_kernel, "sb", (ks, vst, qst, tri_up),
                      [k_spec, vt_spec, qt_spec, _const_spec(tri_up.shape)], [], ks.shape[0], lp)


def _post_kernel(x_ref, meta_ref, of_ref, os_ref, ga_ref, gb_ref, wof_ref, wos_ref, wout_ref, g_ref,
                 h1_ref):
    t = pl.program_id(1)
    h = _tile_input(t, x_ref, meta_ref)
    ya = jnp.dot(of_ref[0], wof_ref[...], preferred_element_type=F32)
    yb = jnp.dot(os_ref[0], wos_ref[...], preferred_element_type=F32)
    merged = (jax.nn.sigmoid(ga_ref[0].astype(F32)) * ya + jax.nn.sigmoid(gb_ref[0].astype(F32)) * yb)
    mixed = jnp.dot(merged.astype(BF16), wout_ref[...], preferred_element_type=F32)
    h1 = h + _rms_norm(mixed, g_ref[...])
    h1_ref[0] = jnp.where(_valid_rows(t, (BLK, 1)), h1, 0.0)


def _post_call(x, meta_blk, o_f, o_s, ga, gb, wof, wos, wout, g1, lp):
    b = x.shape[0]
    nt = lp // BLK
    row_spec = lambda w: pl.BlockSpec((1, BLK, w), lambda bi, t: (bi, t, 0))
    return pl.pallas_call(
        _post_kernel,
        grid=(b, nt),
        in_specs=[_x_spec(), _const_spec(meta_blk.shape), row_spec(WIDTH), row_spec(WIDTH),
                  row_spec(D_MODEL), row_spec(D_MODEL), _const_spec(wof.shape), _const_spec(wos.shape),
                  _const_spec(wout.shape), _const_spec(g1.shape)],
        out_specs=row_spec(D_MODEL),
        out_shape=jax.ShapeDtypeStruct((b, lp, D_MODEL), F32),
        compiler_params=pltpu.CompilerParams(
            dimension_semantics=("arbitrary", "arbitrary"), vmem_limit_bytes=VMEM_LIMIT),
        name="post",
    )(x, meta_blk, o_f, o_s, ga, gb, wof, wos, wout, g1)


def _ffn_kernel(h1_ref, g2_ref, g3_ref, wup_ref, cw_ref, cb_ref, wdn_ref, out_ref, u_ref):
    t = pl.program_id(1)

    @pl.when(t == 0)
    def _():
        u_ref[0:HALO, :] = jnp.zeros((HALO, 2 * D_FF), F32)

    @pl.when(t > 0)
    def _():
        u_ref[0:HALO, :] = u_ref[BLK:BLK + HALO, :]

    h1 = h1_ref[0]
    xn = _rms_norm(h1, g2_ref[...]).astype(BF16)
    u_ref[HALO:HALO + BLK, :] = jnp.dot(xn, wup_ref[...], preferred_element_type=F32)

    def conv(cols):
        out = cb_ref[:, cols]
        for i in range(CONV_WIDTH):
            lo = HALO - (CONV_WIDTH - 1) + i
            out = out + cw_ref[i:i + 1, cols] * u_ref[lo:lo + BLK, cols]
        return out

    ffn = jnp.zeros((BLK, D_MODEL), F32)
    for j in range(D_FF // FF_CHUNK):
        gate = conv(slice(j * FF_CHUNK, (j + 1) * FF_CHUNK))
        val = conv(slice(D_FF + j * FF_CHUNK, D_FF + (j + 1) * FF_CHUNK))
        act = (jax.nn.gelu(gate, approximate=True) * val).astype(BF16)
        ffn = ffn + jnp.dot(act, wdn_ref[j * FF_CHUNK:(j + 1) * FF_CHUNK, :], preferred_element_type=F32)

    out_ref[0] = h1 + _rms_norm(ffn, g3_ref[...])


def _ffn_call(h1, g2, g3, wup, cw, cb, wdn, seq):
    b, lp, _ = h1.shape
    nt = lp // BLK
    return pl.pallas_call(
        _ffn_kernel,
        grid=(b, nt),
        in_specs=[pl.BlockSpec((1, BLK, D_MODEL), lambda bi, t: (bi, t, 0)),
                  _const_spec(g2.shape), _const_spec(g3.shape), _const_spec(wup.shape),
                  _const_spec(cw.shape), _const_spec(cb.shape), _const_spec(wdn.shape)],
        out_specs=pl.BlockSpec((1, BLK, D_MODEL), lambda bi, t: (bi, jnp.maximum(t - NPRE, 0), 0)),
        out_shape=jax.ShapeDtypeStruct((b, seq, D_MODEL), F32),
        scratch_shapes=[pltpu.VMEM((HALO + BLK, 2 * D_FF), F32)],
        compiler_params=pltpu.CompilerParams(
            dimension_semantics=("arbitrary", "arbitrary"), vmem_limit_bytes=VMEM_LIMIT),
        name="ffn",
    )(h1, g2, g3, wup, cw, cb, wdn)


def kernel(x, meta_tokens, norm_gains, w_in, b_forget, w_o_fox, w_o_sb, w_out, w_up, conv_w, conv_b, w_down):
    b, seq, d = x.shape
    assert d == D_MODEL and seq % QBLK == 0 and D_FF % FF_CHUNK == 0 and (QBLK + seq) // BLK <= LANES
    assert norm_gains.shape[0] == 1, "single-layer block"
    lp = QBLK + seq
    scale = HEAD_DIM ** -0.5

    perm = jnp.argsort(b_forget[0])
    by_head = lambda a: a.reshape(D_MODEL, N_HEADS, -1)[:, perm].reshape(D_MODEL, -1)

    w = w_in[0]
    o = 0
    parts = {}
    for name, width in (("qf", WIDTH), ("kf", WIDTH), ("vf", WIDTH), ("f", N_HEADS),
                        ("qs", WIDTH), ("ks", WIDTH), ("vs", WIDTH), ("ga", D_MODEL), ("gb", D_MODEL)):
        parts[name] = w[:, o:o + width]
        o += width
    for name in ("qf", "kf", "vf", "f"):
        parts[name] = by_head(parts[name])
    f_pad = jnp.pad(parts["f"], ((0, 0), (0, LANES - N_HEADS)))
    wn = jnp.concatenate([parts["kf"], parts["ks"], parts["ga"], parts["gb"], f_pad], axis=1).astype(BF16)
    wt = jnp.concatenate([parts["qf"] * scale, parts["vf"], parts["qs"] * scale, parts["vs"]], axis=1).T.astype(BF16)
    bfp = jnp.pad(b_forget[0][perm], (0, LANES - N_HEADS)).reshape(1, LANES)
    wof = w_o_fox[0].reshape(N_HEADS, HEAD_DIM, D_MODEL)[perm].reshape(WIDTH, D_MODEL).astype(BF16)

    meta_blk = jnp.concatenate([jnp.zeros((META_PAD, D_MODEL), x.dtype), meta_tokens.astype(x.dtype)], axis=0)
    gains = norm_gains[0].reshape(4, 1, D_MODEL)

    idx = jnp.arange(BLK)
    tri_low = (idx[None, :] <= idx[:, None]).astype(BF16)
    tri_up = (idx[None, :] >= idx[:, None]).astype(BF16)
    ind = (jnp.arange(WIDTH)[:, None] // HEAD_DIM == jnp.arange(LANES)[None, :]).astype(BF16)
    indt = ind[:, :16].T

    (kf, ks, ga, gb, qft, vft, qst, vst, ka, ct, cf, qn, knmax, cend) = _proj_call(
        x, meta_blk, gains[0], wn, wt, bfp, tri_low, ind, indt, lp)
    cend = jnp.pad(cend[:, KPQ - 1::KPQ, 0, :N_HEADS].transpose(0, 2, 1), ((0, 0), (0, 0), (0, LANES - lp // QBLK)))
    o_f = _fox_call(kf, ka, vft, qft, ct, cf, qn, knmax, cend, lp)
    o_s = _sb_call(ks, vst, qst, tri_up, lp)
    h1 = _post_call(x, meta_blk, o_f, o_s, ga, gb, wof, w_o_sb[0].astype(BF16),
                    w_out[0].astype(BF16), gains[1], lp)
    return _ffn_call(h1, gains[2], gains[3], w_up[0].astype(BF16), conv_w[0], conv_b[0].reshape(1, 2 * D_FF),
                     w_down[0].astype(BF16), seq)
```

```python
import jax
import jax.numpy as jnp
from jax import lax
from jax.experimental import pallas as pl
from jax.experimental.pallas import tpu as pltpu

D_MODEL = 1024
N_META = 16
HEAD_DIM = 64
N_HEADS = 8
WIDTH = N_HEADS * HEAD_DIM
D_FF = 2816
CONV_WIDTH = 3
EPS = 1e-6

BLK = 256
QBLK = 512
ROWS = QBLK
KPQ = QBLK // BLK
PAD = QBLK - N_META
LANES = 128
HALO = 8
FF_CHUNK = 256
NEG = -1e30
SBT = 128
SB_TILES = 3
EXP_UNDERFLOW = -104.0
PAD_BIAS = 32768.0
NORM_SLACK = 1.01
EXIT_SLACK = 2.0
SOFTMAX_ROWS = 64
DENOM_ROWS = 16
VMEM_LIMIT = 56 * 1024 * 1024

F32 = jnp.float32
BF16 = jnp.bfloat16


def _const_spec(shape):
    zeros = (0,) * len(shape)
    return pl.BlockSpec(shape, lambda *_: zeros, pipeline_mode=pl.Buffered(1))


def _rms_norm(v, g):
    ms = jnp.mean(v * v, axis=-1, keepdims=True)
    return v * lax.rsqrt(ms + EPS) * g


def _split3(v):
    hi = v.astype(BF16)
    r1 = v - hi.astype(F32)
    mid = r1.astype(BF16)
    lo = (r1 - mid.astype(F32)).astype(BF16)
    return hi, mid, lo


def _log_sigmoid(v):
    return jnp.minimum(v, 0.0) - jnp.log1p(jnp.exp(-jnp.abs(v)))


def _tile_input(t, x_ref, meta_ref):
    return jnp.where(t == 0, meta_ref[...], x_ref[0])


def _valid_rows(t, shape):
    row = lax.broadcasted_iota(jnp.int32, shape, 0)
    return (t > 0) | (row >= PAD)


def _x_spec():
    return pl.BlockSpec((1, ROWS, D_MODEL), lambda bi, t: (bi, jnp.maximum(t - 1, 0), 0))


def _proj_kernel(x_ref, meta_ref, g_ref, wn_ref, wt_ref, bf_ref, tri_ref, ind_ref, indt_ref,
                 kf_ref, ks_ref, ga_ref, gb_ref, qft_ref, vft_ref, qst_ref, vst_ref,
                 ka_ref, ct_ref, cf_ref, qn_ref, knmax_ref, cend_ref, carry_ref):
    t = pl.program_id(1)

    @pl.when(t == 0)
    def _():
        carry_ref[...] = jnp.zeros_like(carry_ref)
        knmax_ref[...] = jnp.zeros_like(knmax_ref)

    h = _tile_input(t, x_ref, meta_ref)
    xn = _rms_norm(h, g_ref[...]).astype(BF16)

    pn = jnp.dot(xn, wn_ref[...], preferred_element_type=F32)
    kf_ref[0] = pn[:, 0:WIDTH].astype(BF16)
    ks_ref[0] = pn[:, WIDTH:2 * WIDTH].astype(BF16)
    ga_ref[0] = pn[:, 2 * WIDTH:2 * WIDTH + D_MODEL].astype(BF16)
    gb_ref[0] = pn[:, 2 * WIDTH + D_MODEL:2 * WIDTH + 2 * D_MODEL].astype(BF16)

    pt = lax.dot_general(wt_ref[...], xn, (((1,), (1,)), ((), ())),
                         preferred_element_type=F32)
    qft_ref[0] = pt[0:WIDTH].astype(BF16)
    vft_ref[0] = pt[WIDTH:2 * WIDTH].astype(BF16)
    qst_ref[0] = pt[2 * WIDTH:3 * WIDTH].astype(BF16)
    vst_ref[0] = pt[3 * WIDTH:4 * WIDTH].astype(BF16)

    f = pn[:, 2 * WIDTH + 2 * D_MODEL:] + bf_ref[...]
    lane = lax.broadcasted_iota(jnp.int32, (ROWS, LANES), 1)
    valid = (lane < N_HEADS) & _valid_rows(t, (ROWS, LANES))
    logf = jnp.where(valid, _log_sigmoid(f), 0.0)
    tri = tri_ref[...]
    c = carry_ref[...]
    for part in _split3(logf):
        c = c + jnp.dot(tri, part, preferred_element_type=F32)
    carry_ref[...] = c[ROWS - 1:ROWS, :]

    chi, cmid, clo = _split3(c)
    spread = (chi.astype(F32) + pltpu.roll(cmid.astype(F32), N_HEADS, axis=1)
              + pltpu.roll(clo.astype(F32), 2 * N_HEADS, axis=1))
    ones = jnp.where((lane >= 32) & (lane < 32 + 3 * N_HEADS), 1.0, 0.0)
    pad_key = (lane < N_HEADS) & jnp.logical_not(_valid_rows(t, (ROWS, LANES)))
    ka_ref[0] = (ones - jnp.where(pad_key, PAD_BIAS, spread)).astype(BF16)
    ct_ref[0] = spread.T[0:32].astype(BF16)

    cf_ref[0] = c.T[0:N_HEADS]
    cend_ref[0, 0] = jnp.broadcast_to(c[ROWS - 1:ROWS, :], (8, LANES))
    qb = pt[0:WIDTH].astype(BF16).astype(F32)
    qn2 = jnp.dot(indt_ref[...], (qb * qb).astype(BF16), preferred_element_type=F32)
    qn_ref[0] = jnp.sqrt(qn2[0:N_HEADS]) * NORM_SLACK
    kb = pn[:, 0:WIDTH].astype(BF16).astype(F32)
    kn2 = jnp.dot((kb * kb).astype(BF16), ind_ref[...], preferred_element_type=F32)
    kn = jnp.sqrt(jnp.max(kn2, axis=0, keepdims=True)) * NORM_SLACK
    knmax_ref[0] = jnp.maximum(knmax_ref[0], kn)


def _proj_call(x, meta_blk, g0, wn, wt, bfp, tri, ind, indt, lp):
    b = x.shape[0]
    nt = lp // ROWS
    row_spec = lambda w: pl.BlockSpec((1, ROWS, w), lambda bi, t: (bi, t, 0))
    col_spec = lambda r: pl.BlockSpec((1, r, ROWS), lambda bi, t: (bi, 0, t))
    nat = lambda w: jax.ShapeDtypeStruct((b, lp, w), BF16)
    tr = lambda r, dt=BF16: jax.ShapeDtypeStruct((b, r, lp), dt)
    consts = (meta_blk, g0, wn, wt, bfp, tri, ind, indt)
    return pl.pallas_call(
        _proj_kernel,
        grid=(b, nt),
        in_specs=[_x_spec()] + [_const_spec(a.shape) for a in consts],
        out_specs=[row_spec(WIDTH), row_spec(WIDTH), row_spec(D_MODEL), row_spec(D_MODEL),
                   col_spec(WIDTH), col_spec(WIDTH), col_spec(WIDTH), col_spec(WIDTH),
                   row_spec(LANES), col_spec(32), col_spec(N_HEADS), col_spec(N_HEADS),
                   pl.BlockSpec((1, 1, LANES), lambda bi, t: (bi, 0, 0)),
                   pl.BlockSpec((1, 1, 8, LANES), lambda bi, t: (bi, t, 0, 0))],
        out_shape=[nat(WIDTH), nat(WIDTH), nat(D_MODEL), nat(D_MODEL),
                   tr(WIDTH), tr(WIDTH), tr(WIDTH), tr(WIDTH), nat(LANES), tr(32),
                   tr(N_HEADS, F32), tr(N_HEADS, F32),
                   jax.ShapeDtypeStruct((b, 1, LANES), F32),
                   jax.ShapeDtypeStruct((b, nt, 8, LANES), F32)],
        scratch_shapes=[pltpu.VMEM((1, LANES), F32)],
        compiler_params=pltpu.CompilerParams(
            dimension_semantics=("arbitrary", "arbitrary"), vmem_limit_bytes=VMEM_LIMIT),
        name="proj",
    )(x, *consts)


def _tile_mask(kb, qi, strict):
    s_idx = kb * BLK + lax.broadcasted_iota(jnp.int32, (BLK, QBLK), 0)
    t_idx = qi * QBLK + lax.broadcasted_iota(jnp.int32, (BLK, QBLK), 1)
    causal = (s_idx < t_idx) if strict else (s_idx <= t_idx)
    return causal & (s_idx >= PAD)


def _head_rows(qt2, hh):
    z = jnp.zeros((HEAD_DIM, QBLK), BF16)
    q = qt2[hh * HEAD_DIM:(hh + 1) * HEAD_DIM]
    return jnp.concatenate([q, z] if hh == 0 else [z, q], axis=0)


def _fox_kernel(k2_ref, ka_ref, vt2_ref, qt2_ref, ct_ref, cf_ref, qn_ref, knmax_ref, cend_ref,
                o_ref, s_ref, p_ref, acc_ref):
    p = pl.program_id(1)
    qi = pl.program_id(2)
    qt2 = qt2_ref[0]
    ct = ct_ref[0]
    r32 = lax.broadcasted_iota(jnp.int32, (32, QBLK), 0)

    rhs = []
    for hh in range(2):
        sel = (r32 < 3 * N_HEADS) & ((r32 % N_HEADS) == 2 * p + hh)
        rhs.append(jnp.concatenate(
            [_head_rows(qt2, hh),
             jnp.where(sel, 1.0, 0.0).astype(BF16),
             jnp.where(sel, ct, jnp.zeros_like(ct)),
             jnp.zeros((64, QBLK), BF16)], axis=0))

    def scores(kb, diagonal, w, heads=(0, 1)):
        off = pl.multiple_of(kb * QBLK, QBLK)
        lhs = jnp.concatenate([k2_ref[0, pl.ds(off, QBLK), :], ka_ref[0, pl.ds(off, QBLK), :]], axis=1)
        if diagonal:
            ok = (lax.broadcasted_iota(jnp.int32, (QBLK, QBLK), 0)
                  <= lax.broadcasted_iota(jnp.int32, (QBLK, QBLK), 1))
        for hh in heads:
            s = jnp.dot(lhs, rhs[hh], preferred_element_type=F32)
            s_ref[w, hh] = jnp.where(ok, s, NEG) if diagonal else s

    def softmax(b, stats, heads=(0, 1)):
        out = list(stats)
        for hh in heads:
            m = stats[hh][0]
            m_new = jnp.maximum(m, jnp.max(s_ref[b, hh], axis=0, keepdims=True))
            for r in range(0, QBLK, SOFTMAX_ROWS):
                pr = jnp.exp(s_ref[b, hh, r:r + SOFTMAX_ROWS, :] - m_new)
                p_ref[b, hh, r:r + SOFTMAX_ROWS, :] = pr.astype(BF16)
            out[hh] = (m_new, jnp.exp(m - m_new))
        return tuple(out)

    def values(kb, b, stats, heads=(0, 1)):
        off = pl.multiple_of(kb * QBLK, QBLK)
        for hh in heads:
            vt = vt2_ref[0, hh * HEAD_DIM:(hh + 1) * HEAD_DIM, pl.ds(off, QBLK)]
            lhs = jnp.concatenate([vt, jnp.ones((DENOM_ROWS, QBLK), BF16)], axis=0)
            acc_ref[hh] = stats[hh][1] * acc_ref[hh] + jnp.dot(lhs, p_ref[b, hh], preferred_element_type=F32)

    acc_ref[...] = jnp.zeros_like(acc_ref)
    one = (jnp.full((1, QBLK), NEG, F32), jnp.ones((1, QBLK), F32))

    def single_block(_):
        scores(0, True, 0)
        stats = softmax(0, (one, one))
        values(0, 0, stats)
        return stats

    def sweep(_):
        scores(qi, True, 0)
        scores(qi - 1, False, 1)
        stats = softmax(0, (one, one))

        lane = lax.broadcasted_iota(jnp.int32, (1, LANES), 1)
        need = jnp.float32(0.0)
        for hh in range(2):
            head = 2 * p + hh
            kn = jnp.max(jnp.where(lane == head, knmax_ref[0], 0.0), axis=1, keepdims=True)
            bound = qn_ref[0, pl.ds(head, 1), :] * kn + cf_ref[0, pl.ds(head, 1), :] - stats[hh][0]
            limit = jnp.max(bound, axis=1, keepdims=True) - EXP_UNDERFLOW + EXIT_SLACK
            live = (cend_ref[0, pl.ds(head, 1), :] <= limit) & (lane <= qi - 2)
            need = jnp.maximum(need, jnp.sum(jnp.where(live, 1.0, 0.0)))
        n_blocks = 2 + need.astype(jnp.int32)

        def body(k, cur, st):
            scores(qi - k, False, cur, (0,))
            first = softmax(1 - cur, st, (0,))
            scores(qi - k, False, cur, (1,))
            values(qi - k + 2, cur, st, (0,))
            second = softmax(1 - cur, st, (1,))
            values(qi - k + 2, cur, st, (1,))
            return first[0], second[1]

        def drain(cur, st):
            last = softmax(1 - cur, st)
            values(qi - n_blocks + 2, cur, st)
            values(qi - n_blocks + 1, 1 - cur, last)
            return last

        def by_parity(k, fn, st):
            return lax.cond(lax.bitwise_and(k, 1) == 0, lambda s: fn(0, s), lambda s: fn(1, s), st)

        stats = lax.fori_loop(2, n_blocks, lambda k, st: by_parity(k, lambda c, s: body(k, c, s), st), stats)
        return by_parity(n_blocks, drain, stats)

    lax.cond(qi == 0, single_block, sweep, 0)
    rows = [acc_ref[hh, 0:HEAD_DIM, :] / acc_ref[hh, HEAD_DIM:HEAD_DIM + 1, :] for hh in range(2)]
    o_ref[0] = jnp.concatenate(rows, axis=0).T.astype(BF16)


def _sb_kernel(k2_ref, vt2_ref, qt2_ref, tt_ref, o_ref):
    assert KPQ == 2 and PAD // BLK == 1 and (SB_TILES - 1) * SBT == BLK
    qi = pl.program_id(2)
    qt2 = qt2_ref[0]
    rhs = [_head_rows(qt2, hh) for hh in range(2)]
    tt = tt_ref[...]
    top = KPQ * qi + 1
    nsub = QBLK // SBT

    def log_keep(z):
        return jnp.minimum(-z, 0.0) - jnp.log(1.0 + jnp.exp(-jnp.abs(z)))

    def block(z, ok, cr, tri):
        lk = log_keep(z)
        if ok is not None:
            lk = jnp.where(ok, lk, 0.0)
        lkb = lk.astype(BF16)
        later = jnp.dot(tri, lkb, preferred_element_type=F32) + cr
        a = jnp.exp((z + lk) + later)
        if ok is not None:
            a = jnp.where(ok, a, 0.0)
        return a.astype(BF16), later[0:1, :] + lkb[0:1, :].astype(F32)

    def first_block(_):
        half_ok = _tile_mask(top, qi, strict=True)[:, BLK:]
        off = pl.multiple_of(top * BLK, BLK)
        out = []
        for hh in range(2):
            z = jnp.dot(k2_ref[0, pl.ds(off, BLK), :], rhs[hh][:, BLK:], preferred_element_type=F32)
            a, cr = block(z, half_ok, jnp.zeros((1, BLK), F32), tt)
            a = jnp.concatenate([jnp.zeros((BLK, BLK), BF16), a], axis=1)
            cr = jnp.concatenate([jnp.zeros((1, BLK), F32), cr], axis=1)
            vt = vt2_ref[0, hh * HEAD_DIM:(hh + 1) * HEAD_DIM, pl.ds(off, BLK)]
            out.append((cr, jnp.dot(vt, a, preferred_element_type=F32)))
        return tuple(out)

    def window(pad_keys):
        def run(_):
            n_tiles = nsub + SB_TILES - 1
            off0 = pl.multiple_of(qi * QBLK - (SB_TILES - 1) * SBT, SBT)
            kwin = k2_ref[0, pl.ds(off0, n_tiles * SBT), :]
            row = lax.broadcasted_iota(jnp.int32, (SBT, QBLK), 0)
            col = lax.broadcasted_iota(jnp.int32, (SBT, QBLK), 1)
            causal = row < lax.bitwise_and(col, SBT - 1)
            tri = tt[:SBT, :SBT]
            zs = [jnp.dot(kwin, rhs[hh], preferred_element_type=F32) for hh in range(2)]
            crs = [jnp.zeros((1, QBLK), F32) for _ in range(2)]
            weights = [[], []]
            for age in range(SB_TILES):
                first = SB_TILES - 1 - age
                ok = causal if age == 0 else None
                if pad_keys:
                    real = off0 + first * SBT + lax.bitwise_and(col, -SBT) + row >= PAD
                    ok = real if ok is None else ok & real
                for hh in range(2):
                    z_age = jnp.concatenate(
                        [zs[hh][(first + j) * SBT:(first + j + 1) * SBT, j * SBT:(j + 1) * SBT]
                         for j in range(nsub)], axis=1)
                    a, crs[hh] = block(z_age, ok, crs[hh], tri)
                    weights[hh].append(a)
            zero = jnp.zeros((SBT, SBT), BF16)
            out = []
            for hh in range(2):
                band = jnp.concatenate(
                    [jnp.concatenate(
                        [weights[hh][SB_TILES - 1 + j - i][:, j * SBT:(j + 1) * SBT]
                         if 0 <= SB_TILES - 1 + j - i < SB_TILES else zero for j in range(nsub)], axis=1)
                     for i in range(n_tiles)], axis=0)
                vt = vt2_ref[0, hh * HEAD_DIM:(hh + 1) * HEAD_DIM, pl.ds(off0, n_tiles * SBT)]
                out.append((crs[hh], jnp.dot(vt, band, preferred_element_type=F32)))
            return tuple(out)
        return run

    state = lax.switch(jnp.minimum(qi, 2), [first_block, window(True), window(False)], 0)

    def tile(kb, mask, st):
        off = pl.multiple_of(kb * BLK, BLK)
        k2 = k2_ref[0, pl.ds(off, BLK), :]
        ok = None
        if mask == "pad":
            ok = _tile_mask(kb, qi, strict=True)
        elif mask == "before_window":
            s_idx = kb * BLK + lax.broadcasted_iota(jnp.int32, (BLK, QBLK), 0)
            col = lax.broadcasted_iota(jnp.int32, (BLK, QBLK), 1)
            lo = qi * QBLK + lax.bitwise_and(col, -SBT) - (SB_TILES - 1) * SBT
            ok = (s_idx < lo) & (s_idx >= PAD)
        out = []
        for hh in range(2):
            cr, acc = st[hh]
            a, cr = block(jnp.dot(k2, rhs[hh], preferred_element_type=F32), ok, cr, tt)
            vt = vt2_ref[0, hh * HEAD_DIM:(hh + 1) * HEAD_DIM, pl.ds(off, BLK)]
            out.append((cr, acc + jnp.dot(vt, a, preferred_element_type=F32)))
        return tuple(out)

    def live(st):
        return jnp.maximum(jnp.max(st[0][0]), jnp.max(st[1][0])) >= EXP_UNDERFLOW

    state = lax.cond((qi >= 1) & live(state),
                     lambda st: tile(top - 2, "before_window", tile(top - 1, "before_window", st)),
                     lambda st: st, state)
    kb, state = lax.while_loop(lambda c: (c[0] >= 2) & live(c[1]),
                               lambda c: (c[0] - 1, tile(c[0], None, c[1])),
                               (top - 3, state))
    state = lax.cond((kb == 1) & live(state), lambda st: tile(1, "pad", st), lambda st: st, state)
    o_ref[0] = jnp.concatenate([state[0][1], state[1][1]], axis=0).T.astype(BF16)


def _attn_call(kernel, name, args, in_specs, scratch, b, lp):
    return pl.pallas_call(
        kernel,
        grid=(b, N_HEADS // 2, lp // QBLK),
        in_specs=in_specs,
        out_specs=pl.BlockSpec((1, QBLK, LANES), lambda bi, p, qi: (bi, qi, p)),
        out_shape=jax.ShapeDtypeStruct((b, lp, WIDTH), BF16),
        scratch_shapes=scratch,
        compiler_params=pltpu.CompilerParams(
            dimension_semantics=("arbitrary", "arbitrary", "arbitrary"), vmem_limit_bytes=VMEM_LIMIT),
        name=name,
    )(*args)


def _pair_specs(lp):
    k_spec = pl.BlockSpec((1, lp, LANES), lambda bi, p, qi: (bi, 0, p))
    vt_spec = pl.BlockSpec((1, LANES, lp), lambda bi, p, qi: (bi, p, 0))
    qt_spec = pl.BlockSpec((1, LANES, QBLK), lambda bi, p, qi: (bi, p, qi))
    return k_spec, vt_spec, qt_spec


def _fox_call(kf, ka, vft, qft, ct, cf, qn, knmax, cend, lp):
    k_spec, vt_spec, qt_spec = _pair_specs(lp)
    ka_spec = pl.BlockSpec((1, lp, LANES), lambda bi, p, qi: (bi, 0, 0))
    ct_spec = pl.BlockSpec((1, 32, QBLK), lambda bi, p, qi: (bi, 0, qi))
    head_spec = pl.BlockSpec((1, N_HEADS, QBLK), lambda bi, p, qi: (bi, 0, qi))
    knmax_spec = pl.BlockSpec((1, 1, LANES), lambda bi, p, qi: (bi, 0, 0))
    cend_spec = pl.BlockSpec((1, N_HEADS, LANES), lambda bi, p, qi: (bi, 0, 0))
    scratch = [pltpu.VMEM((2, 2, QBLK, QBLK), F32),
               pltpu.VMEM((2, 2, QBLK, QBLK), BF16),
               pltpu.VMEM((2, HEAD_DIM + DENOM_ROWS, QBLK), F32)]
    return _attn_call(_fox_kernel, "fox", (kf, ka, vft, qft, ct, cf, qn, knmax, cend),
                      [k_spec, ka_spec, vt_spec, qt_spec, ct_spec, head_spec, head_spec, knmax_spec, cend_spec],
                      scratch, kf.shape[0], lp)


def _sb_call(ks, vst, qst, tri_up, lp):
    k_spec, vt_spec, qt_spec = _pair_specs(lp)
    return _attn_call(_sb_kernel, "sb", (ks, vst, qst, tri_up),
                      [k_spec, vt_spec, qt_spec, _const_spec(tri_up.shape)], [], ks.shape[0], lp)


def _post_kernel(x_ref, meta_ref, of_ref, os_ref, ga_ref, gb_ref, wof_ref, wos_ref, wout_ref, g_ref,
                 h1_ref):
    t = pl.program_id(1)
    h = _tile_input(t, x_ref, meta_ref)
    ya = jnp.dot(of_ref[0], wof_ref[...], preferred_element_type=F32)
    yb = jnp.dot(os_ref[0], wos_ref[...], preferred_element_type=F32)
    merged = (jax.nn.sigmoid(ga_ref[0].astype(F32)) * ya + jax.nn.sigmoid(gb_ref[0].astype(F32)) * yb)
    mixed = jnp.dot(merged.astype(BF16), wout_ref[...], preferred_element_type=F32)
    h1 = h + _rms_norm(mixed, g_ref[...])
    h1_ref[0] = jnp.where(_valid_rows(t, (ROWS, 1)), h1, 0.0)


def _post_call(x, meta_blk, o_f, o_s, ga, gb, wof, wos, wout, g1, lp):
    b = x.shape[0]
    nt = lp // ROWS
    row_spec = lambda w: pl.BlockSpec((1, ROWS, w), lambda bi, t: (bi, t, 0))
    return pl.pallas_call(
        _post_kernel,
        grid=(b, nt),
        in_specs=[_x_spec(), _const_spec(meta_blk.shape), row_spec(WIDTH), row_spec(WIDTH),
                  row_spec(D_MODEL), row_spec(D_MODEL), _const_spec(wof.shape), _const_spec(wos.shape),
                  _const_spec(wout.shape), _const_spec(g1.shape)],
        out_specs=row_spec(D_MODEL),
        out_shape=jax.ShapeDtypeStruct((b, lp, D_MODEL), F32),
        compiler_params=pltpu.CompilerParams(
            dimension_semantics=("arbitrary", "arbitrary"), vmem_limit_bytes=VMEM_LIMIT),
        name="post",
    )(x, meta_blk, o_f, o_s, ga, gb, wof, wos, wout, g1)


def _ffn_kernel(h1_ref, g2_ref, g3_ref, wup_ref, cw_ref, cb_ref, wdn_ref, out_ref, u_ref):
    t = pl.program_id(1)

    @pl.when(t == 0)
    def _():
        u_ref[0:HALO, :] = jnp.zeros((HALO, 2 * D_FF), F32)

    @pl.when(t > 0)
    def _():
        u_ref[0:HALO, :] = u_ref[ROWS:ROWS + HALO, :]

    h1 = h1_ref[0]
    xn = _rms_norm(h1, g2_ref[...]).astype(BF16)
    u_ref[HALO:HALO + ROWS, :] = jnp.dot(xn, wup_ref[...], preferred_element_type=F32)

    def conv(cols):
        out = cb_ref[:, cols]
        for i in range(CONV_WIDTH):
            lo = HALO - (CONV_WIDTH - 1) + i
            out = out + cw_ref[i:i + 1, cols] * u_ref[lo:lo + ROWS, cols]
        return out

    ffn = jnp.zeros((ROWS, D_MODEL), F32)
    for j in range(D_FF // FF_CHUNK):
        gate = conv(slice(j * FF_CHUNK, (j + 1) * FF_CHUNK))
        val = conv(slice(D_FF + j * FF_CHUNK, D_FF + (j + 1) * FF_CHUNK))
        act = (jax.nn.gelu(gate, approximate=True) * val).astype(BF16)
        ffn = ffn + jnp.dot(act, wdn_ref[j * FF_CHUNK:(j + 1) * FF_CHUNK, :], preferred_element_type=F32)

    out_ref[0] = h1 + _rms_norm(ffn, g3_ref[...])


def _ffn_call(h1, g2, g3, wup, cw, cb, wdn, seq):
    b, lp, _ = h1.shape
    nt = lp // ROWS
    return pl.pallas_call(
        _ffn_kernel,
        grid=(b, nt),
        in_specs=[pl.BlockSpec((1, ROWS, D_MODEL), lambda bi, t: (bi, t, 0)),
                  _const_spec(g2.shape), _const_spec(g3.shape), _const_spec(wup.shape),
                  _const_spec(cw.shape), _const_spec(cb.shape), _const_spec(wdn.shape)],
        out_specs=pl.BlockSpec((1, ROWS, D_MODEL), lambda bi, t: (bi, jnp.maximum(t - 1, 0), 0)),
        out_shape=jax.ShapeDtypeStruct((b, seq, D_MODEL), F32),
        scratch_shapes=[pltpu.VMEM((HALO + ROWS, 2 * D_FF), F32)],
        compiler_params=pltpu.CompilerParams(
            dimension_semantics=("arbitrary", "arbitrary"), vmem_limit_bytes=VMEM_LIMIT),
        name="ffn",
    )(h1, g2, g3, wup, cw, cb, wdn)


def kernel(x, meta_tokens, norm_gains, w_in, b_forget, w_o_fox, w_o_sb, w_out, w_up, conv_w, conv_b, w_down):
    b, seq, d = x.shape
    assert d == D_MODEL and seq % QBLK == 0 and D_FF % FF_CHUNK == 0 and (QBLK + seq) // QBLK <= LANES
    assert norm_gains.shape[0] == 1, "single-layer block"
    lp = QBLK + seq
    scale = HEAD_DIM ** -0.5

    perm = jnp.argsort(b_forget[0])
    by_head = lambda a: a.reshape(D_MODEL, N_HEADS, -1)[:, perm].reshape(D_MODEL, -1)

    w = w_in[0]
    o = 0
    parts = {}
    for name, width in (("qf", WIDTH), ("kf", WIDTH), ("vf", WIDTH), ("f", N_HEADS),
                        ("qs", WIDTH), ("ks", WIDTH), ("vs", WIDTH), ("ga", D_MODEL), ("gb", D_MODEL)):
        parts[name] = w[:, o:o + width]
        o += width
    for name in ("qf", "kf", "vf", "f"):
        parts[name] = by_head(parts[name])
    f_pad = jnp.pad(parts["f"], ((0, 0), (0, LANES - N_HEADS)))
    wn = jnp.concatenate([parts["kf"], parts["ks"], parts["ga"], parts["gb"], f_pad], axis=1).astype(BF16)
    wt = jnp.concatenate([parts["qf"] * scale, parts["vf"], parts["qs"] * scale, parts["vs"]], axis=1).T.astype(BF16)
    bfp = jnp.pad(b_forget[0][perm], (0, LANES - N_HEADS)).reshape(1, LANES)
    wof = w_o_fox[0].reshape(N_HEADS, HEAD_DIM, D_MODEL)[perm].reshape(WIDTH, D_MODEL).astype(BF16)

    meta_blk = jnp.concatenate([jnp.zeros((PAD, D_MODEL), x.dtype), meta_tokens.astype(x.dtype)], axis=0)
    gains = norm_gains[0].reshape(4, 1, D_MODEL)

    tri = lambda n, op: op(jnp.arange(n)[None, :], jnp.arange(n)[:, None]).astype(BF16)
    tri_low = tri(ROWS, jnp.less_equal)
    tri_up = tri(BLK, jnp.greater)
    ind = (jnp.arange(WIDTH)[:, None] // HEAD_DIM == jnp.arange(LANES)[None, :]).astype(BF16)
    indt = ind[:, :16].T

    (kf, ks, ga, gb, qft, vft, qst, vst, ka, ct, cf, qn, knmax, cend) = _proj_call(
        x, meta_blk, gains[0], wn, wt, bfp, tri_low, ind, indt, lp)
    cend = jnp.pad(cend[:, :, 0, :N_HEADS].transpose(0, 2, 1), ((0, 0), (0, 0), (0, LANES - lp // QBLK)))
    o_f = _fox_call(kf, ka, vft, qft, ct, cf, qn, knmax, cend, lp)
    o_s = _sb_call(ks, vst, qst, tri_up, lp)
    h1 = _post_call(x, meta_blk, o_f, o_s, ga, gb, wof, w_o_sb[0].astype(BF16),
                    w_out[0].astype(BF16), gains[1], lp)
    return _ffn_call(h1, gains[2], gains[3], w_up[0].astype(BF16), conv_w[0], conv_b[0].reshape(1, 2 * D_FF),
                     w_down[0].astype(BF16), seq)
```

```python
import jax
import jax.numpy as jnp
from jax import lax
from jax.experimental import pallas as pl
from jax.experimental.pallas import tpu as pltpu

D_MODEL = 1024
N_META = 16
HEAD_DIM = 64
N_HEADS = 8
WIDTH = N_HEADS * HEAD_DIM
D_FF = 2816
CONV_WIDTH = 3
EPS = 1e-6

BLK = 256
QBLK = 512
ROWS = QBLK
KPQ = QBLK // BLK
PAD = QBLK - N_META
LANES = 128
HALO = 8
FF_CHUNK = 256
NEG = -1e30
SBT = 128
SB_TILES = 3
EXP_UNDERFLOW = -104.0
PAD_BIAS = 32768.0
NORM_SLACK = 1.01
EXIT_SLACK = 2.0
SOFTMAX_ROWS = 64
DENOM_ROWS = 16
VMEM_LIMIT = 56 * 1024 * 1024

F32 = jnp.float32
BF16 = jnp.bfloat16


def _const_spec(shape):
    zeros = (0,) * len(shape)
    return pl.BlockSpec(shape, lambda *_: zeros, pipeline_mode=pl.Buffered(1))


def _rms_norm(v, g):
    ms = jnp.mean(v * v, axis=-1, keepdims=True)
    return v * lax.rsqrt(ms + EPS) * g


def _split3(v):
    hi = v.astype(BF16)
    r1 = v - hi.astype(F32)
    mid = r1.astype(BF16)
    lo = (r1 - mid.astype(F32)).astype(BF16)
    return hi, mid, lo


def _log_sigmoid(v):
    return jnp.minimum(v, 0.0) - jnp.log1p(jnp.exp(-jnp.abs(v)))


def _tile_input(t, x_ref, meta_ref):
    return jnp.where(t == 0, meta_ref[...], x_ref[0])


def _valid_rows(t, shape):
    row = lax.broadcasted_iota(jnp.int32, shape, 0)
    return (t > 0) | (row >= PAD)


def _x_spec():
    return pl.BlockSpec((1, ROWS, D_MODEL), lambda bi, t: (bi, jnp.maximum(t - 1, 0), 0))


def _proj_kernel(x_ref, meta_ref, g_ref, wn_ref, wt_ref, bf_ref, tri_ref, ind_ref, indt_ref,
                 kf_ref, ks_ref, ga_ref, gb_ref, qft_ref, vft_ref, qst_ref, vst_ref,
                 ka_ref, ct_ref, cf_ref, qn_ref, knmax_ref, cend_ref, carry_ref):
    t = pl.program_id(1)

    @pl.when(t == 0)
    def _():
        carry_ref[...] = jnp.zeros_like(carry_ref)
        knmax_ref[...] = jnp.zeros_like(knmax_ref)

    h = _tile_input(t, x_ref, meta_ref)
    xn = _rms_norm(h, g_ref[...]).astype(BF16)

    pn = jnp.dot(xn, wn_ref[...], preferred_element_type=F32)
    kf_ref[0] = pn[:, 0:WIDTH].astype(BF16)
    ks_ref[0] = pn[:, WIDTH:2 * WIDTH].astype(BF16)
    ga_ref[0] = pn[:, 2 * WIDTH:2 * WIDTH + D_MODEL].astype(BF16)
    gb_ref[0] = pn[:, 2 * WIDTH + D_MODEL:2 * WIDTH + 2 * D_MODEL].astype(BF16)

    pt = lax.dot_general(wt_ref[...], xn, (((1,), (1,)), ((), ())),
                         preferred_element_type=F32)
    qft_ref[0] = pt[0:WIDTH].astype(BF16)
    vft_ref[0] = pt[WIDTH:2 * WIDTH].astype(BF16)
    qst_ref[0] = pt[2 * WIDTH:3 * WIDTH].astype(BF16)
    vst_ref[0] = pt[3 * WIDTH:4 * WIDTH].astype(BF16)

    f = pn[:, 2 * WIDTH + 2 * D_MODEL:] + bf_ref[...]
    lane = lax.broadcasted_iota(jnp.int32, (ROWS, LANES), 1)
    valid = (lane < N_HEADS) & _valid_rows(t, (ROWS, LANES))
    logf = jnp.where(valid, _log_sigmoid(f), 0.0)
    tri = tri_ref[...]
    c = carry_ref[...]
    for part in _split3(logf):
        c = c + jnp.dot(tri, part, preferred_element_type=F32)
    carry_ref[...] = c[ROWS - 1:ROWS, :]

    chi, cmid, clo = _split3(c)
    spread = (chi.astype(F32) + pltpu.roll(cmid.astype(F32), N_HEADS, axis=1)
              + pltpu.roll(clo.astype(F32), 2 * N_HEADS, axis=1))
    ones = jnp.where((lane >= 32) & (lane < 32 + 3 * N_HEADS), 1.0, 0.0)
    pad_key = (lane < N_HEADS) & jnp.logical_not(_valid_rows(t, (ROWS, LANES)))
    ka_ref[0] = (ones - jnp.where(pad_key, PAD_BIAS, spread)).astype(BF16)
    ct_ref[0] = spread.T[0:32].astype(BF16)

    cf_ref[0] = c.T[0:N_HEADS]
    cend_ref[0, 0] = jnp.broadcast_to(c[ROWS - 1:ROWS, :], (8, LANES))
    qb = pt[0:WIDTH].astype(BF16).astype(F32)
    qn2 = jnp.dot(indt_ref[...], (qb * qb).astype(BF16), preferred_element_type=F32)
    qn_ref[0] = jnp.sqrt(qn2[0:N_HEADS]) * NORM_SLACK
    kb = pn[:, 0:WIDTH].astype(BF16).astype(F32)
    kn2 = jnp.dot((kb * kb).astype(BF16), ind_ref[...], preferred_element_type=F32)
    kn = jnp.sqrt(jnp.max(kn2, axis=0, keepdims=True)) * NORM_SLACK
    knmax_ref[0] = jnp.maximum(knmax_ref[0], kn)


def _proj_call(x, meta_blk, g0, wn, wt, bfp, tri, ind, indt, lp):
    b = x.shape[0]
    nt = lp // ROWS
    row_spec = lambda w: pl.BlockSpec((1, ROWS, w), lambda bi, t: (bi, t, 0))
    col_spec = lambda r: pl.BlockSpec((1, r, ROWS), lambda bi, t: (bi, 0, t))
    nat = lambda w: jax.ShapeDtypeStruct((b, lp, w), BF16)
    tr = lambda r, dt=BF16: jax.ShapeDtypeStruct((b, r, lp), dt)
    consts = (meta_blk, g0, wn, wt, bfp, tri, ind, indt)
    return pl.pallas_call(
        _proj_kernel,
        grid=(b, nt),
        in_specs=[_x_spec()] + [_const_spec(a.shape) for a in consts],
        out_specs=[row_spec(WIDTH), row_spec(WIDTH), row_spec(D_MODEL), row_spec(D_MODEL),
                   col_spec(WIDTH), col_spec(WIDTH), col_spec(WIDTH), col_spec(WIDTH),
                   row_spec(LANES), col_spec(32), col_spec(N_HEADS), col_spec(N_HEADS),
                   pl.BlockSpec((1, 1, LANES), lambda bi, t: (bi, 0, 0)),
                   pl.BlockSpec((1, 1, 8, LANES), lambda bi, t: (bi, t, 0, 0))],
        out_shape=[nat(WIDTH), nat(WIDTH), nat(D_MODEL), nat(D_MODEL),
                   tr(WIDTH), tr(WIDTH), tr(WIDTH), tr(WIDTH), nat(LANES), tr(32),
                   tr(N_HEADS, F32), tr(N_HEADS, F32),
                   jax.ShapeDtypeStruct((b, 1, LANES), F32),
                   jax.ShapeDtypeStruct((b, nt, 8, LANES), F32)],
        scratch_shapes=[pltpu.VMEM((1, LANES), F32)],
        compiler_params=pltpu.CompilerParams(
            dimension_semantics=("arbitrary", "arbitrary"), vmem_limit_bytes=VMEM_LIMIT),
        name="proj",
    )(x, *consts)


def _tile_mask(kb, qi, strict):
    s_idx = kb * BLK + lax.broadcasted_iota(jnp.int32, (BLK, QBLK), 0)
    t_idx = qi * QBLK + lax.broadcasted_iota(jnp.int32, (BLK, QBLK), 1)
    causal = (s_idx < t_idx) if strict else (s_idx <= t_idx)
    return causal & (s_idx >= PAD)


def _head_rows(qt2, hh):
    z = jnp.zeros((HEAD_DIM, QBLK), BF16)
    q = qt2[hh * HEAD_DIM:(hh + 1) * HEAD_DIM]
    return jnp.concatenate([q, z] if hh == 0 else [z, q], axis=0)


def _fox_kernel(k2_ref, ka_ref, vt2_ref, qt2_ref, ct_ref, cf_ref, qn_ref, knmax_ref, cend_ref,
                o_ref, s_ref, p_ref, acc_ref):
    p = pl.program_id(1)
    qi = pl.program_id(2)
    qt2 = qt2_ref[0]
    ct = ct_ref[0]
    r32 = lax.broadcasted_iota(jnp.int32, (32, QBLK), 0)

    rhs = []
    for hh in range(2):
        sel = (r32 < 3 * N_HEADS) & ((r32 % N_HEADS) == 2 * p + hh)
        rhs.append(jnp.concatenate(
            [_head_rows(qt2, hh),
             jnp.where(sel, 1.0, 0.0).astype(BF16),
             jnp.where(sel, ct, jnp.zeros_like(ct)),
             jnp.zeros((64, QBLK), BF16)], axis=0))

    def scores(kb, diagonal, w, heads=(0, 1)):
        off = pl.multiple_of(kb * QBLK, QBLK)
        lhs = jnp.concatenate([k2_ref[0, pl.ds(off, QBLK), :], ka_ref[0, pl.ds(off, QBLK), :]], axis=1)
        if diagonal:
            ok = (lax.broadcasted_iota(jnp.int32, (QBLK, QBLK), 0)
                  <= lax.broadcasted_iota(jnp.int32, (QBLK, QBLK), 1))
        for hh in heads:
            s = jnp.dot(lhs, rhs[hh], preferred_element_type=F32)
            s_ref[w, hh] = jnp.where(ok, s, NEG) if diagonal else s

    def softmax(b, stats, heads=(0, 1)):
        out = list(stats)
        for hh in heads:
            m = stats[hh][0]
            m_new = jnp.maximum(m, jnp.max(s_ref[b, hh], axis=0, keepdims=True))
            for r in range(0, QBLK, SOFTMAX_ROWS):
                pr = jnp.exp(s_ref[b, hh, r:r + SOFTMAX_ROWS, :] - m_new)
                p_ref[b, hh, r:r + SOFTMAX_ROWS, :] = pr.astype(BF16)
            out[hh] = (m_new, jnp.exp(m - m_new))
        return tuple(out)

    def values(kb, b, stats, heads=(0, 1)):
        off = pl.multiple_of(kb * QBLK, QBLK)
        for hh in heads:
            vt = vt2_ref[0, hh * HEAD_DIM:(hh + 1) * HEAD_DIM, pl.ds(off, QBLK)]
            lhs = jnp.concatenate([vt, jnp.ones((DENOM_ROWS, QBLK), BF16)], axis=0)
            acc_ref[hh] = stats[hh][1] * acc_ref[hh] + jnp.dot(lhs, p_ref[b, hh], preferred_element_type=F32)

    acc_ref[...] = jnp.zeros_like(acc_ref)
    one = (jnp.full((1, QBLK), NEG, F32), jnp.ones((1, QBLK), F32))

    def single_block(_):
        scores(0, True, 0)
        stats = softmax(0, (one, one))
        values(0, 0, stats)
        return stats

    def sweep(_):
        scores(qi, True, 0)
        scores(qi - 1, False, 1)
        stats = softmax(0, (one, one))

        lane = lax.broadcasted_iota(jnp.int32, (1, LANES), 1)
        need = jnp.float32(0.0)
        for hh in range(2):
            head = 2 * p + hh
            kn = jnp.max(jnp.where(lane == head, knmax_ref[0], 0.0), axis=1, keepdims=True)
            bound = qn_ref[0, pl.ds(head, 1), :] * kn + cf_ref[0, pl.ds(head, 1), :] - stats[hh][0]
            limit = jnp.max(bound, axis=1, keepdims=True) - EXP_UNDERFLOW + EXIT_SLACK
            live = (cend_ref[0, pl.ds(head, 1), :] <= limit) & (lane <= qi - 2)
            need = jnp.maximum(need, jnp.sum(jnp.where(live, 1.0, 0.0)))
        n_blocks = 2 + need.astype(jnp.int32)

        def body(k, cur, st):
            scores(qi - k, False, cur, (0,))
            first = softmax(1 - cur, st, (0,))
            scores(qi - k, False, cur, (1,))
            values(qi - k + 2, cur, st, (0,))
            second = softmax(1 - cur, st, (1,))
            values(qi - k + 2, cur, st, (1,))
            return first[0], second[1]

        def drain(cur, st):
            last = softmax(1 - cur, st)
            values(qi - n_blocks + 2, cur, st)
            values(qi - n_blocks + 1, 1 - cur, last)
            return last

        def by_parity(k, fn, st):
            return lax.cond(lax.bitwise_and(k, 1) == 0, lambda s: fn(0, s), lambda s: fn(1, s), st)

        stats = lax.fori_loop(2, n_blocks, lambda k, st: by_parity(k, lambda c, s: body(k, c, s), st), stats)
        return by_parity(n_blocks, drain, stats)

    lax.cond(qi == 0, single_block, sweep, 0)
    rows = [acc_ref[hh, 0:HEAD_DIM, :] / acc_ref[hh, HEAD_DIM:HEAD_DIM + 1, :] for hh in range(2)]
    o_ref[0] = jnp.concatenate(rows, axis=0).T.astype(BF16)


def _sb_kernel(k2_ref, vt2_ref, qt2_ref, tt_ref, o_ref):
    assert KPQ == 2 and PAD // BLK == 1 and (SB_TILES - 1) * SBT == BLK
    qi = pl.program_id(2)
    qt2 = qt2_ref[0]
    rhs = [_head_rows(qt2, hh) for hh in range(2)]
    tt = tt_ref[...]
    top = KPQ * qi + 1
    nsub = QBLK // SBT

    def log_keep(z):
        return jnp.minimum(-z, 0.0) - jnp.log(1.0 + jnp.exp(-jnp.abs(z)))

    def block(z, ok, cr, tri):
        lk = log_keep(z)
        if ok is not None:
            lk = jnp.where(ok, lk, 0.0)
        lkb = lk.astype(BF16)
        later = jnp.dot(tri, lkb, preferred_element_type=F32) + cr
        a = jnp.exp((z + lk) + later)
        if ok is not None:
            a = jnp.where(ok, a, 0.0)
        return a.astype(BF16), later[0:1, :] + lkb[0:1, :].astype(F32)

    def first_block(_):
        half_ok = _tile_mask(top, qi, strict=True)[:, BLK:]
        off = pl.multiple_of(top * BLK, BLK)
        out = []
        for hh in range(2):
            z = jnp.dot(k2_ref[0, pl.ds(off, BLK), :], rhs[hh][:, BLK:], preferred_element_type=F32)
            a, cr = block(z, half_ok, jnp.zeros((1, BLK), F32), tt)
            a = jnp.concatenate([jnp.zeros((BLK, BLK), BF16), a], axis=1)
            cr = jnp.concatenate([jnp.zeros((1, BLK), F32), cr], axis=1)
            vt = vt2_ref[0, hh * HEAD_DIM:(hh + 1) * HEAD_DIM, pl.ds(off, BLK)]
            out.append((cr, jnp.dot(vt, a, preferred_element_type=F32)))
        return tuple(out)

    def window(pad_keys):
        def run(_):
            n_tiles = nsub + SB_TILES - 1
            off0 = pl.multiple_of(qi * QBLK - (SB_TILES - 1) * SBT, SBT)
            kwin = k2_ref[0, pl.ds(off0, n_tiles * SBT), :]
            row = lax.broadcasted_iota(jnp.int32, (SBT, QBLK), 0)
            col = lax.broadcasted_iota(jnp.int32, (SBT, QBLK), 1)
            causal = row < lax.bitwise_and(col, SBT - 1)
            tri = tt[:SBT, :SBT]
            zs = [jnp.dot(kwin, rhs[hh], preferred_element_type=F32) for hh in range(2)]
            crs = [jnp.zeros((1, QBLK), F32) for _ in range(2)]
            weights = [[], []]
            for age in range(SB_TILES):
                first = SB_TILES - 1 - age
                ok = causal if age == 0 else None
                if pad_keys:
                    real = off0 + first * SBT + lax.bitwise_and(col, -SBT) + row >= PAD
                    ok = real if ok is None else ok & real
                for hh in range(2):
                    z_age = jnp.concatenate(
                        [zs[hh][(first + j) * SBT:(first + j + 1) * SBT, j * SBT:(j + 1) * SBT]
                         for j in range(nsub)], axis=1)
                    a, crs[hh] = block(z_age, ok, crs[hh], tri)
                    weights[hh].append(a)
            zero = jnp.zeros((SBT, SBT), BF16)
            out = []
            for hh in range(2):
                band = jnp.concatenate(
                    [jnp.concatenate(
                        [weights[hh][SB_TILES - 1 + j - i][:, j * SBT:(j + 1) * SBT]
                         if 0 <= SB_TILES - 1 + j - i < SB_TILES else zero for j in range(nsub)], axis=1)
                     for i in range(n_tiles)], axis=0)
                vt = vt2_ref[0, hh * HEAD_DIM:(hh + 1) * HEAD_DIM, pl.ds(off0, n_tiles * SBT)]
                out.append((crs[hh], jnp.dot(vt, band, preferred_element_type=F32)))
            return tuple(out)
        return run

    state = lax.switch(jnp.minimum(qi, 2), [first_block, window(True), window(False)], 0)

    def tile(kb, mask, st):
        off = pl.multiple_of(kb * BLK, BLK)
        k2 = k2_ref[0, pl.ds(off, BLK), :]
        ok = None
        if mask == "pad":
            ok = _tile_mask(kb, qi, strict=True)
        elif mask == "before_window":
            s_idx = kb * BLK + lax.broadcasted_iota(jnp.int32, (BLK, QBLK), 0)
            col = lax.broadcasted_iota(jnp.int32, (BLK, QBLK), 1)
            lo = qi * QBLK + lax.bitwise_and(col, -SBT) - (SB_TILES - 1) * SBT
            ok = (s_idx < lo) & (s_idx >= PAD)
        out = []
        for hh in range(2):
            cr, acc = st[hh]
            a, cr = block(jnp.dot(k2, rhs[hh], preferred_element_type=F32), ok, cr, tt)
            vt = vt2_ref[0, hh * HEAD_DIM:(hh + 1) * HEAD_DIM, pl.ds(off, BLK)]
            out.append((cr, acc + jnp.dot(vt, a, preferred_element_type=F32)))
        return tuple(out)

    def live(st):
        return jnp.maximum(jnp.max(st[0][0]), jnp.max(st[1][0])) >= EXP_UNDERFLOW

    def older_keys(st):
        st = tile(top - 2, "before_window", tile(top - 1, "before_window", st))
        kb, st = lax.while_loop(lambda c: (c[0] >= 2) & live(c[1]),
                                lambda c: (c[0] - 1, tile(c[0], None, c[1])),
                                (top - 3, st))
        return lax.cond((kb == 1) & live(st), lambda s_: tile(1, "pad", s_), lambda s_: s_, st)

    state = lax.cond((qi >= 1) & live(state), older_keys, lambda st: st, state)
    o_ref[0] = jnp.concatenate([state[0][1], state[1][1]], axis=0).T.astype(BF16)


def _attn_call(kernel, name, args, in_specs, scratch, b, lp):
    return pl.pallas_call(
        kernel,
        grid=(b, N_HEADS // 2, lp // QBLK),
        in_specs=in_specs,
        out_specs=pl.BlockSpec((1, QBLK, LANES), lambda bi, p, qi: (bi, qi, p)),
        out_shape=jax.ShapeDtypeStruct((b, lp, WIDTH), BF16),
        scratch_shapes=scratch,
        compiler_params=pltpu.CompilerParams(
            dimension_semantics=("arbitrary", "arbitrary", "arbitrary"), vmem_limit_bytes=VMEM_LIMIT),
        name=name,
    )(*args)


def _pair_specs(lp):
    k_spec = pl.BlockSpec((1, lp, LANES), lambda bi, p, qi: (bi, 0, p))
    vt_spec = pl.BlockSpec((1, LANES, lp), lambda bi, p, qi: (bi, p, 0))
    qt_spec = pl.BlockSpec((1, LANES, QBLK), lambda bi, p, qi: (bi, p, qi))
    return k_spec, vt_spec, qt_spec


def _fox_call(kf, ka, vft, qft, ct, cf, qn, knmax, cend, lp):
    k_spec, vt_spec, qt_spec = _pair_specs(lp)
    ka_spec = pl.BlockSpec((1, lp, LANES), lambda bi, p, qi: (bi, 0, 0))
    ct_spec = pl.BlockSpec((1, 32, QBLK), lambda bi, p, qi: (bi, 0, qi))
    head_spec = pl.BlockSpec((1, N_HEADS, QBLK), lambda bi, p, qi: (bi, 0, qi))
    knmax_spec = pl.BlockSpec((1, 1, LANES), lambda bi, p, qi: (bi, 0, 0))
    cend_spec = pl.BlockSpec((1, N_HEADS, LANES), lambda bi, p, qi: (bi, 0, 0))
    scratch = [pltpu.VMEM((2, 2, QBLK, QBLK), F32),
               pltpu.VMEM((2, 2, QBLK, QBLK), BF16),
               pltpu.VMEM((2, HEAD_DIM + DENOM_ROWS, QBLK), F32)]
    return _attn_call(_fox_kernel, "fox", (kf, ka, vft, qft, ct, cf, qn, knmax, cend),
                      [k_spec, ka_spec, vt_spec, qt_spec, ct_spec, head_spec, head_spec, knmax_spec, cend_spec],
                      scratch, kf.shape[0], lp)


def _sb_call(ks, vst, qst, tri_up, lp):
    k_spec, vt_spec, qt_spec = _pair_specs(lp)
    return _attn_call(_sb_kernel, "sb", (ks, vst, qst, tri_up),
                      [k_spec, vt_spec, qt_spec, _const_spec(tri_up.shape)], [], ks.shape[0], lp)


def _post_kernel(x_ref, meta_ref, of_ref, os_ref, ga_ref, gb_ref, wof_ref, wos_ref, wout_ref, g_ref,
                 h1_ref):
    t = pl.program_id(1)

    def mix(h, rows):
        ya = jnp.dot(of_ref[0, rows, :], wof_ref[...], preferred_element_type=F32)
        yb = jnp.dot(os_ref[0, rows, :], wos_ref[...], preferred_element_type=F32)
        merged = (jax.nn.sigmoid(ga_ref[0, rows, :].astype(F32)) * ya
                  + jax.nn.sigmoid(gb_ref[0, rows, :].astype(F32)) * yb)
        mixed = jnp.dot(merged.astype(BF16), wout_ref[...], preferred_element_type=F32)
        return h + _rms_norm(mixed, g_ref[...])

    @pl.when(t == 0)
    def _():
        meta_rows = slice(PAD, ROWS)
        h1_ref[0, 0:PAD, :] = jnp.zeros((PAD, D_MODEL), F32)
        h1_ref[0, meta_rows, :] = mix(meta_ref[meta_rows, :], meta_rows)

    @pl.when(t > 0)
    def _():
        h1_ref[0] = mix(x_ref[0], slice(None))


def _post_call(x, meta_blk, o_f, o_s, ga, gb, wof, wos, wout, g1, lp):
    b = x.shape[0]
    nt = lp // ROWS
    row_spec = lambda w: pl.BlockSpec((1, ROWS, w), lambda bi, t: (bi, t, 0))
    return pl.pallas_call(
        _post_kernel,
        grid=(b, nt),
        in_specs=[_x_spec(), _const_spec(meta_blk.shape), row_spec(WIDTH), row_spec(WIDTH),
                  row_spec(D_MODEL), row_spec(D_MODEL), _const_spec(wof.shape), _const_spec(wos.shape),
                  _const_spec(wout.shape), _const_spec(g1.shape)],
        out_specs=row_spec(D_MODEL),
        out_shape=jax.ShapeDtypeStruct((b, lp, D_MODEL), F32),
        compiler_params=pltpu.CompilerParams(
            dimension_semantics=("arbitrary", "arbitrary"), vmem_limit_bytes=VMEM_LIMIT),
        name="post",
    )(x, meta_blk, o_f, o_s, ga, gb, wof, wos, wout, g1)


def _ffn_kernel(h1_ref, g2_ref, g3_ref, wup_ref, cw_ref, cb_ref, wdn_ref, out_ref, u_ref):
    t = pl.program_id(1)

    def up_projection(rows):
        xn = _rms_norm(h1_ref[0, rows, :], g2_ref[...]).astype(BF16)
        return jnp.dot(xn, wup_ref[...], preferred_element_type=F32)

    @pl.when(t == 0)
    def _():
        u_ref[HALO + PAD:HALO + ROWS, :] = up_projection(slice(PAD, ROWS))

    @pl.when(t > 0)
    def _():
        u_ref[0:HALO, :] = u_ref[ROWS:ROWS + HALO, :]
        u_ref[HALO:HALO + ROWS, :] = up_projection(slice(None))

        def conv(cols):
            out = cb_ref[:, cols]
            for i in range(CONV_WIDTH):
                lo = HALO - (CONV_WIDTH - 1) + i
                out = out + cw_ref[i:i + 1, cols] * u_ref[lo:lo + ROWS, cols]
            return out

        ffn = jnp.zeros((ROWS, D_MODEL), F32)
        for j in range(D_FF // FF_CHUNK):
            gate = conv(slice(j * FF_CHUNK, (j + 1) * FF_CHUNK))
            val = conv(slice(D_FF + j * FF_CHUNK, D_FF + (j + 1) * FF_CHUNK))
            act = (jax.nn.gelu(gate, approximate=True) * val).astype(BF16)
            ffn = ffn + jnp.dot(act, wdn_ref[j * FF_CHUNK:(j + 1) * FF_CHUNK, :], preferred_element_type=F32)

        out_ref[0] = h1_ref[0] + _rms_norm(ffn, g3_ref[...])


def _ffn_call(h1, g2, g3, wup, cw, cb, wdn, seq):
    b, lp, _ = h1.shape
    nt = lp // ROWS
    return pl.pallas_call(
        _ffn_kernel,
        grid=(b, nt),
        in_specs=[pl.BlockSpec((1, ROWS, D_MODEL), lambda bi, t: (bi, t, 0)),
                  _const_spec(g2.shape), _const_spec(g3.shape), _const_spec(wup.shape),
                  _const_spec(cw.shape), _const_spec(cb.shape), _const_spec(wdn.shape)],
        out_specs=pl.BlockSpec((1, ROWS, D_MODEL), lambda bi, t: (bi, jnp.maximum(t - 1, 0), 0)),
        out_shape=jax.ShapeDtypeStruct((b, seq, D_MODEL), F32),
        scratch_shapes=[pltpu.VMEM((HALO + ROWS, 2 * D_FF), F32)],
        compiler_params=pltpu.CompilerParams(
            dimension_semantics=("arbitrary", "arbitrary"), vmem_limit_bytes=VMEM_LIMIT),
        name="ffn",
    )(h1, g2, g3, wup, cw, cb, wdn)


def kernel(x, meta_tokens, norm_gains, w_in, b_forget, w_o_fox, w_o_sb, w_out, w_up, conv_w, conv_b, w_down):
    b, seq, d = x.shape
    assert d == D_MODEL and seq % QBLK == 0 and D_FF % FF_CHUNK == 0 and (QBLK + seq) // QBLK <= LANES
    assert norm_gains.shape[0] == 1, "single-layer block"
    lp = QBLK + seq
    scale = HEAD_DIM ** -0.5

    perm = jnp.argsort(b_forget[0])
    by_head = lambda a: a.reshape(D_MODEL, N_HEADS, -1)[:, perm].reshape(D_MODEL, -1)

    w = w_in[0]
    o = 0
    parts = {}
    for name, width in (("qf", WIDTH), ("kf", WIDTH), ("vf", WIDTH), ("f", N_HEADS),
                        ("qs", WIDTH), ("ks", WIDTH), ("vs", WIDTH), ("ga", D_MODEL), ("gb", D_MODEL)):
        parts[name] = w[:, o:o + width]
        o += width
    for name in ("qf", "kf", "vf", "f"):
        parts[name] = by_head(parts[name])
    f_pad = jnp.pad(parts["f"], ((0, 0), (0, LANES - N_HEADS)))
    wn = jnp.concatenate([parts["kf"], parts["ks"], parts["ga"], parts["gb"], f_pad], axis=1).astype(BF16)
    wt = jnp.concatenate([parts["qf"] * scale, parts["vf"], parts["qs"] * scale, parts["vs"]], axis=1).T.astype(BF16)
    bfp = jnp.pad(b_forget[0][perm], (0, LANES - N_HEADS)).reshape(1, LANES)
    wof = w_o_fox[0].reshape(N_HEADS, HEAD_DIM, D_MODEL)[perm].reshape(WIDTH, D_MODEL).astype(BF16)

    meta_blk = jnp.concatenate([jnp.zeros((PAD, D_MODEL), x.dtype), meta_tokens.astype(x.dtype)], axis=0)
    gains = norm_gains[0].reshape(4, 1, D_MODEL)

    tri = lambda n, op: op(jnp.arange(n)[None, :], jnp.arange(n)[:, None]).astype(BF16)
    tri_low = tri(ROWS, jnp.less_equal)
    tri_up = tri(BLK, jnp.greater)
    ind = (jnp.arange(WIDTH)[:, None] // HEAD_DIM == jnp.arange(LANES)[None, :]).astype(BF16)
    indt = ind[:, :16].T

    (kf, ks, ga, gb, qft, vft, qst, vst, ka, ct, cf, qn, knmax, cend) = _proj_call(
        x, meta_blk, gains[0], wn, wt, bfp, tri_low, ind, indt, lp)
    cend = jnp.pad(cend[:, :, 0, :N_HEADS].transpose(0, 2, 1), ((0, 0), (0, 0), (0, LANES - lp // QBLK)))
    o_f = _fox_call(kf, ka, vft, qft, ct, cf, qn, knmax, cend, lp)
    o_s = _sb_call(ks, vst, qst, tri_up, lp)
    h1 = _post_call(x, meta_blk, o_f, o_s, ga, gb, wof, w_o_sb[0].astype(BF16),
                    w_out[0].astype(BF16), gains[1], lp)
    return _ffn_call(h1, gains[2], gains[3], w_up[0].astype(BF16), conv_w[0], conv_b[0].reshape(1, 2 * D_FF),
                     w_down[0].astype(BF16), seq)
```

```python
import jax
import jax.numpy as jnp
from jax import lax
from jax.experimental import pallas as pl
from jax.experimental.pallas import tpu as pltpu

D_MODEL = 1024
N_META = 16
HEAD_DIM = 64
N_HEADS = 8
WIDTH = N_HEADS * HEAD_DIM
D_FF = 2816
CONV_WIDTH = 3
EPS = 1e-6

BLK = 256
QBLK = 512
ROWS = QBLK
KPQ = QBLK // BLK
PAD = QBLK - N_META
LANES = 128
HALO = 8
FF_CHUNK = 256
NEG = -1e30
SBT = 128
SB_TILES = 3
EXP_UNDERFLOW = -104.0
PAD_BIAS = 32768.0
NORM_SLACK = 1.01
EXIT_SLACK = 2.0
SOFTMAX_ROWS = 64
DENOM_ROWS = 16
VMEM_LIMIT = 56 * 1024 * 1024

F32 = jnp.float32
BF16 = jnp.bfloat16


def _const_spec(shape):
    zeros = (0,) * len(shape)
    return pl.BlockSpec(shape, lambda *_: zeros, pipeline_mode=pl.Buffered(1))


def _rms_norm(v, g):
    ms = jnp.mean(v * v, axis=-1, keepdims=True)
    return v * lax.rsqrt(ms + EPS) * g


def _split3(v):
    hi = v.astype(BF16)
    r1 = v - hi.astype(F32)
    mid = r1.astype(BF16)
    lo = (r1 - mid.astype(F32)).astype(BF16)
    return hi, mid, lo


def _log_sigmoid(v):
    return jnp.minimum(v, 0.0) - jnp.log1p(jnp.exp(-jnp.abs(v)))


def _tile_input(t, x_ref, meta_ref):
    return jnp.where(t == 0, meta_ref[...], x_ref[0])


def _valid_rows(t, shape):
    row = lax.broadcasted_iota(jnp.int32, shape, 0)
    return (t > 0) | (row >= PAD)


def _x_spec():
    return pl.BlockSpec((1, ROWS, D_MODEL), lambda bi, t: (bi, jnp.maximum(t - 1, 0), 0))


def _proj_kernel(x_ref, meta_ref, g_ref, wn_ref, wt_ref, bf_ref, tri_ref, ind_ref, indt_ref,
                 kf_ref, ks_ref, ga_ref, gb_ref, qft_ref, vft_ref, qst_ref, vst_ref,
                 ka_ref, ct_ref, cf_ref, qn_ref, knmax_ref, cend_ref, carry_ref):
    t = pl.program_id(1)

    @pl.when(t == 0)
    def _():
        carry_ref[...] = jnp.zeros_like(carry_ref)
        knmax_ref[...] = jnp.zeros_like(knmax_ref)

    h = _tile_input(t, x_ref, meta_ref)
    xn = _rms_norm(h, g_ref[...]).astype(BF16)

    pn = jnp.dot(xn, wn_ref[...], preferred_element_type=F32)
    kf_ref[0] = pn[:, 0:WIDTH].astype(BF16)
    ks_ref[0] = pn[:, WIDTH:2 * WIDTH].astype(BF16)
    ga_ref[0] = pn[:, 2 * WIDTH:2 * WIDTH + D_MODEL].astype(BF16)
    gb_ref[0] = pn[:, 2 * WIDTH + D_MODEL:2 * WIDTH + 2 * D_MODEL].astype(BF16)

    pt = lax.dot_general(wt_ref[...], xn, (((1,), (1,)), ((), ())),
                         preferred_element_type=F32)
    qft_ref[0] = pt[0:WIDTH].astype(BF16)
    vft_ref[0] = pt[WIDTH:2 * WIDTH].astype(BF16)
    qst_ref[0] = pt[2 * WIDTH:3 * WIDTH].astype(BF16)
    vst_ref[0] = pt[3 * WIDTH:4 * WIDTH].astype(BF16)

    f = pn[:, 2 * WIDTH + 2 * D_MODEL:] + bf_ref[...]
    lane = lax.broadcasted_iota(jnp.int32, (ROWS, LANES), 1)
    valid = (lane < N_HEADS) & _valid_rows(t, (ROWS, LANES))
    logf = jnp.where(valid, _log_sigmoid(f), 0.0)
    tri = tri_ref[...]
    c = carry_ref[...]
    for part in _split3(logf):
        c = c + jnp.dot(tri, part, preferred_element_type=F32)
    carry_ref[...] = c[ROWS - 1:ROWS, :]

    chi, cmid, clo = _split3(c)
    spread = (chi.astype(F32) + pltpu.roll(cmid.astype(F32), N_HEADS, axis=1)
              + pltpu.roll(clo.astype(F32), 2 * N_HEADS, axis=1))
    ones = jnp.where((lane >= 32) & (lane < 32 + 3 * N_HEADS), 1.0, 0.0)
    pad_key = (lane < N_HEADS) & jnp.logical_not(_valid_rows(t, (ROWS, LANES)))
    ka_ref[0] = (ones - jnp.where(pad_key, PAD_BIAS, spread)).astype(BF16)
    ct_ref[0] = spread.T[0:32].astype(BF16)

    cf_ref[0] = c.T[0:N_HEADS]
    cend_ref[0, 0] = jnp.broadcast_to(c[ROWS - 1:ROWS, :], (8, LANES))
    qb = pt[0:WIDTH].astype(BF16).astype(F32)
    qn2 = jnp.dot(indt_ref[...], (qb * qb).astype(BF16), preferred_element_type=F32)
    qn_ref[0] = jnp.sqrt(qn2[0:N_HEADS]) * NORM_SLACK
    kb = pn[:, 0:WIDTH].astype(BF16).astype(F32)
    kn2 = jnp.dot((kb * kb).astype(BF16), ind_ref[...], preferred_element_type=F32)
    kn = jnp.sqrt(jnp.max(kn2, axis=0, keepdims=True)) * NORM_SLACK
    knmax_ref[0] = jnp.maximum(knmax_ref[0], kn)


def _proj_call(x, meta_blk, g0, wn, wt, bfp, tri, ind, indt, lp):
    b = x.shape[0]
    nt = lp // ROWS
    row_spec = lambda w: pl.BlockSpec((1, ROWS, w), lambda bi, t: (bi, t, 0))
    col_spec = lambda r: pl.BlockSpec((1, r, ROWS), lambda bi, t: (bi, 0, t))
    nat = lambda w: jax.ShapeDtypeStruct((b, lp, w), BF16)
    tr = lambda r, dt=BF16: jax.ShapeDtypeStruct((b, r, lp), dt)
    consts = (meta_blk, g0, wn, wt, bfp, tri, ind, indt)
    return pl.pallas_call(
        _proj_kernel,
        grid=(b, nt),
        in_specs=[_x_spec()] + [_const_spec(a.shape) for a in consts],
        out_specs=[row_spec(WIDTH), row_spec(WIDTH), row_spec(D_MODEL), row_spec(D_MODEL),
                   col_spec(WIDTH), col_spec(WIDTH), col_spec(WIDTH), col_spec(WIDTH),
                   row_spec(LANES), col_spec(32), col_spec(N_HEADS), col_spec(N_HEADS),
                   pl.BlockSpec((1, 1, LANES), lambda bi, t: (bi, 0, 0)),
                   pl.BlockSpec((1, 1, 8, LANES), lambda bi, t: (bi, t, 0, 0))],
        out_shape=[nat(WIDTH), nat(WIDTH), nat(D_MODEL), nat(D_MODEL),
                   tr(WIDTH), tr(WIDTH), tr(WIDTH), tr(WIDTH), nat(LANES), tr(32),
                   tr(N_HEADS, F32), tr(N_HEADS, F32),
                   jax.ShapeDtypeStruct((b, 1, LANES), F32),
                   jax.ShapeDtypeStruct((b, nt, 8, LANES), F32)],
        scratch_shapes=[pltpu.VMEM((1, LANES), F32)],
        compiler_params=pltpu.CompilerParams(
            dimension_semantics=("arbitrary", "arbitrary"), vmem_limit_bytes=VMEM_LIMIT),
        name="proj",
    )(x, *consts)


def _tile_mask(kb, qi, strict):
    s_idx = kb * BLK + lax.broadcasted_iota(jnp.int32, (BLK, QBLK), 0)
    t_idx = qi * QBLK + lax.broadcasted_iota(jnp.int32, (BLK, QBLK), 1)
    causal = (s_idx < t_idx) if strict else (s_idx <= t_idx)
    return causal & (s_idx >= PAD)


def _head_rows(qt2, hh):
    z = jnp.zeros((HEAD_DIM, QBLK), BF16)
    q = qt2[hh * HEAD_DIM:(hh + 1) * HEAD_DIM]
    return jnp.concatenate([q, z] if hh == 0 else [z, q], axis=0)


def _fox_kernel(k2_ref, ka_ref, vt2_ref, qt2_ref, ct_ref, cf_ref, qn_ref, knmax_ref, cend_ref,
                o_ref, s_ref, p_ref, acc_ref):
    refs = (k2_ref, ka_ref, vt2_ref, qt2_ref, ct_ref, cf_ref, qn_ref, knmax_ref, cend_ref,
            o_ref, s_ref, p_ref, acc_ref)

    def query_block(qi, carry):
        _fox_query_block(qi, *refs)
        return carry

    lax.fori_loop(0, qt2_ref.shape[2] // QBLK, query_block, 0)


def _fox_query_block(qi, k2_ref, ka_ref, vt2_ref, qt2_ref, ct_ref, cf_ref, qn_ref, knmax_ref, cend_ref,
                     o_ref, s_ref, p_ref, acc_ref):
    p = pl.program_id(1)
    qoff = pl.multiple_of(qi * QBLK, QBLK)
    qt2 = qt2_ref[0, :, pl.ds(qoff, QBLK)]
    ct = ct_ref[0, :, pl.ds(qoff, QBLK)]
    r32 = lax.broadcasted_iota(jnp.int32, (32, QBLK), 0)

    rhs = []
    for hh in range(2):
        sel = (r32 < 3 * N_HEADS) & ((r32 % N_HEADS) == 2 * p + hh)
        rhs.append(jnp.concatenate(
            [_head_rows(qt2, hh),
             jnp.where(sel, 1.0, 0.0).astype(BF16),
             jnp.where(sel, ct, jnp.zeros_like(ct)),
             jnp.zeros((64, QBLK), BF16)], axis=0))

    def scores(kb, diagonal, w, heads=(0, 1)):
        off = pl.multiple_of(kb * QBLK, QBLK)
        lhs = jnp.concatenate([k2_ref[0, pl.ds(off, QBLK), :], ka_ref[0, pl.ds(off, QBLK), :]], axis=1)
        if diagonal:
            ok = (lax.broadcasted_iota(jnp.int32, (QBLK, QBLK), 0)
                  <= lax.broadcasted_iota(jnp.int32, (QBLK, QBLK), 1))
        for hh in heads:
            s = jnp.dot(lhs, rhs[hh], preferred_element_type=F32)
            s_ref[w, hh] = jnp.where(ok, s, NEG) if diagonal else s

    def softmax(b, stats, heads=(0, 1)):
        out = list(stats)
        for hh in heads:
            m = stats[hh][0]
            m_new = jnp.maximum(m, jnp.max(s_ref[b, hh], axis=0, keepdims=True))
            for r in range(0, QBLK, SOFTMAX_ROWS):
                pr = jnp.exp(s_ref[b, hh, r:r + SOFTMAX_ROWS, :] - m_new)
                p_ref[b, hh, r:r + SOFTMAX_ROWS, :] = pr.astype(BF16)
            out[hh] = (m_new, jnp.exp(m - m_new))
        return tuple(out)

    def values(kb, b, stats, heads=(0, 1)):
        off = pl.multiple_of(kb * QBLK, QBLK)
        for hh in heads:
            vt = vt2_ref[0, hh * HEAD_DIM:(hh + 1) * HEAD_DIM, pl.ds(off, QBLK)]
            lhs = jnp.concatenate([vt, jnp.ones((DENOM_ROWS, QBLK), BF16)], axis=0)
            acc_ref[hh] = stats[hh][1] * acc_ref[hh] + jnp.dot(lhs, p_ref[b, hh], preferred_element_type=F32)

    acc_ref[...] = jnp.zeros_like(acc_ref)
    one = (jnp.full((1, QBLK), NEG, F32), jnp.ones((1, QBLK), F32))

    def single_block(_):
        scores(0, True, 0)
        stats = softmax(0, (one, one))
        values(0, 0, stats)
        return stats

    def sweep(_):
        scores(qi, True, 0)
        scores(qi - 1, False, 1)
        stats = softmax(0, (one, one))

        lane = lax.broadcasted_iota(jnp.int32, (1, LANES), 1)
        need = jnp.float32(0.0)
        for hh in range(2):
            head = 2 * p + hh
            kn = jnp.max(jnp.where(lane == head, knmax_ref[0], 0.0), axis=1, keepdims=True)
            bound = (qn_ref[0, pl.ds(head, 1), pl.ds(qoff, QBLK)] * kn
                     + cf_ref[0, pl.ds(head, 1), pl.ds(qoff, QBLK)] - stats[hh][0])
            limit = jnp.max(bound, axis=1, keepdims=True) - EXP_UNDERFLOW + EXIT_SLACK
            live = (cend_ref[0, pl.ds(head, 1), :] <= limit) & (lane <= qi - 2)
            need = jnp.maximum(need, jnp.sum(jnp.where(live, 1.0, 0.0)))
        n_blocks = 2 + need.astype(jnp.int32)

        def body(k, cur, st):
            scores(qi - k, False, cur, (0,))
            first = softmax(1 - cur, st, (0,))
            scores(qi - k, False, cur, (1,))
            values(qi - k + 2, cur, st, (0,))
            second = softmax(1 - cur, st, (1,))
            values(qi - k + 2, cur, st, (1,))
            return first[0], second[1]

        def drain(cur, st):
            last = softmax(1 - cur, st)
            values(qi - n_blocks + 2, cur, st)
            values(qi - n_blocks + 1, 1 - cur, last)
            return last

        def by_parity(k, fn, st):
            return lax.cond(lax.bitwise_and(k, 1) == 0, lambda s: fn(0, s), lambda s: fn(1, s), st)

        stats = lax.fori_loop(2, n_blocks, lambda k, st: by_parity(k, lambda c, s: body(k, c, s), st), stats)
        return by_parity(n_blocks, drain, stats)

    lax.cond(qi == 0, single_block, sweep, 0)
    rows = [acc_ref[hh, 0:HEAD_DIM, :] / acc_ref[hh, HEAD_DIM:HEAD_DIM + 1, :] for hh in range(2)]
    o_ref[0, pl.ds(qoff, QBLK), :] = jnp.concatenate(rows, axis=0).T.astype(BF16)


def _sb_kernel(k2_ref, vt2_ref, qt2_ref, tt_ref, o_ref):
    def query_block(qi, carry):
        _sb_query_block(qi, k2_ref, vt2_ref, qt2_ref, tt_ref, o_ref)
        return carry

    lax.fori_loop(0, qt2_ref.shape[2] // QBLK, query_block, 0)


def _sb_query_block(qi, k2_ref, vt2_ref, qt2_ref, tt_ref, o_ref):
    assert KPQ == 2 and PAD // BLK == 1 and (SB_TILES - 1) * SBT == BLK
    qoff = pl.multiple_of(qi * QBLK, QBLK)
    qt2 = qt2_ref[0, :, pl.ds(qoff, QBLK)]
    rhs = [_head_rows(qt2, hh) for hh in range(2)]
    tt = tt_ref[...]
    top = KPQ * qi + 1
    nsub = QBLK // SBT

    def log_keep(z):
        return jnp.minimum(-z, 0.0) - jnp.log(1.0 + jnp.exp(-jnp.abs(z)))

    def block(z, ok, cr, tri):
        lk = log_keep(z)
        if ok is not None:
            lk = jnp.where(ok, lk, 0.0)
        lkb = lk.astype(BF16)
        later = jnp.dot(tri, lkb, preferred_element_type=F32) + cr
        a = jnp.exp((z + lk) + later)
        if ok is not None:
            a = jnp.where(ok, a, 0.0)
        return a.astype(BF16), later[0:1, :] + lkb[0:1, :].astype(F32)

    def first_block(_):
        half_ok = _tile_mask(top, qi, strict=True)[:, BLK:]
        off = pl.multiple_of(top * BLK, BLK)
        out = []
        for hh in range(2):
            z = jnp.dot(k2_ref[0, pl.ds(off, BLK), :], rhs[hh][:, BLK:], preferred_element_type=F32)
            a, cr = block(z, half_ok, jnp.zeros((1, BLK), F32), tt)
            a = jnp.concatenate([jnp.zeros((BLK, BLK), BF16), a], axis=1)
            cr = jnp.concatenate([jnp.zeros((1, BLK), F32), cr], axis=1)
            vt = vt2_ref[0, hh * HEAD_DIM:(hh + 1) * HEAD_DIM, pl.ds(off, BLK)]
            out.append((cr, jnp.dot(vt, a, preferred_element_type=F32)))
        return tuple(out)

    def window(pad_keys):
        def run(_):
            n_tiles = nsub + SB_TILES - 1
            off0 = pl.multiple_of(qi * QBLK - (SB_TILES - 1) * SBT, SBT)
            kwin = k2_ref[0, pl.ds(off0, n_tiles * SBT), :]
            row = lax.broadcasted_iota(jnp.int32, (SBT, QBLK), 0)
            col = lax.broadcasted_iota(jnp.int32, (SBT, QBLK), 1)
            causal = row < lax.bitwise_and(col, SBT - 1)
            tri = tt[:SBT, :SBT]
            zs = [jnp.dot(kwin, rhs[hh], preferred_element_type=F32) for hh in range(2)]
            crs = [jnp.zeros((1, QBLK), F32) for _ in range(2)]
            weights = [[], []]
            for age in range(SB_TILES):
                first = SB_TILES - 1 - age
                ok = causal if age == 0 else None
                if pad_keys:
                    real = off0 + first * SBT + lax.bitwise_and(col, -SBT) + row >= PAD
                    ok = real if ok is None else ok & real
                for hh in range(2):
                    z_age = jnp.concatenate(
                        [zs[hh][(first + j) * SBT:(first + j + 1) * SBT, j * SBT:(j + 1) * SBT]
                         for j in range(nsub)], axis=1)
                    a, crs[hh] = block(z_age, ok, crs[hh], tri)
                    weights[hh].append(a)
            zero = jnp.zeros((SBT, SBT), BF16)
            out = []
            for hh in range(2):
                band = jnp.concatenate(
                    [jnp.concatenate(
                        [weights[hh][SB_TILES - 1 + j - i][:, j * SBT:(j + 1) * SBT]
                         if 0 <= SB_TILES - 1 + j - i < SB_TILES else zero for j in range(nsub)], axis=1)
                     for i in range(n_tiles)], axis=0)
                vt = vt2_ref[0, hh * HEAD_DIM:(hh + 1) * HEAD_DIM, pl.ds(off0, n_tiles * SBT)]
                out.append((crs[hh], jnp.dot(vt, band, preferred_element_type=F32)))
            return tuple(out)
        return run

    state = lax.switch(jnp.minimum(qi, 2), [first_block, window(True), window(False)], 0)

    def tile(kb, mask, st):
        off = pl.multiple_of(kb * BLK, BLK)
        k2 = k2_ref[0, pl.ds(off, BLK), :]
        ok = None
        if mask == "pad":
            ok = _tile_mask(kb, qi, strict=True)
        elif mask == "before_window":
            s_idx = kb * BLK + lax.broadcasted_iota(jnp.int32, (BLK, QBLK), 0)
            col = lax.broadcasted_iota(jnp.int32, (BLK, QBLK), 1)
            lo = qi * QBLK + lax.bitwise_and(col, -SBT) - (SB_TILES - 1) * SBT
            ok = (s_idx < lo) & (s_idx >= PAD)
        out = []
        for hh in range(2):
            cr, acc = st[hh]
            a, cr = block(jnp.dot(k2, rhs[hh], preferred_element_type=F32), ok, cr, tt)
            vt = vt2_ref[0, hh * HEAD_DIM:(hh + 1) * HEAD_DIM, pl.ds(off, BLK)]
            out.append((cr, acc + jnp.dot(vt, a, preferred_element_type=F32)))
        return tuple(out)

    def live(st):
        return jnp.maximum(jnp.max(st[0][0]), jnp.max(st[1][0])) >= EXP_UNDERFLOW

    def older_keys(st):
        st = tile(top - 2, "before_window", tile(top - 1, "before_window", st))
        kb, st = lax.while_loop(lambda c: (c[0] >= 2) & live(c[1]),
                                lambda c: (c[0] - 1, tile(c[0], None, c[1])),
                                (top - 3, st))
        return lax.cond((kb == 1) & live(st), lambda s_: tile(1, "pad", s_), lambda s_: s_, st)

    state = lax.cond((qi >= 1) & live(state), older_keys, lambda st: st, state)
    o_ref[0, pl.ds(qoff, QBLK), :] = jnp.concatenate([state[0][1], state[1][1]], axis=0).T.astype(BF16)


def _attn_call(kernel, name, args, in_specs, scratch, b, lp):
    return pl.pallas_call(
        kernel,
        grid=(b, N_HEADS // 2),
        in_specs=in_specs,
        out_specs=pl.BlockSpec((1, lp, LANES), lambda bi, p: (bi, 0, p)),
        out_shape=jax.ShapeDtypeStruct((b, lp, WIDTH), BF16),
        scratch_shapes=scratch,
        compiler_params=pltpu.CompilerParams(
            dimension_semantics=("arbitrary", "arbitrary"), vmem_limit_bytes=VMEM_LIMIT),
        name=name,
    )(*args)


def _pair_specs(lp):
    k_spec = pl.BlockSpec((1, lp, LANES), lambda bi, p: (bi, 0, p))
    vt_spec = pl.BlockSpec((1, LANES, lp), lambda bi, p: (bi, p, 0))
    qt_spec = vt_spec
    return k_spec, vt_spec, qt_spec


def _fox_call(kf, ka, vft, qft, ct, cf, qn, knmax, cend, lp):
    k_spec, vt_spec, qt_spec = _pair_specs(lp)
    ka_spec = pl.BlockSpec((1, lp, LANES), lambda bi, p: (bi, 0, 0))
    ct_spec = pl.BlockSpec((1, 32, lp), lambda bi, p: (bi, 0, 0))
    head_spec = pl.BlockSpec((1, N_HEADS, lp), lambda bi, p: (bi, 0, 0))
    knmax_spec = pl.BlockSpec((1, 1, LANES), lambda bi, p: (bi, 0, 0))
    cend_spec = pl.BlockSpec((1, N_HEADS, LANES), lambda bi, p: (bi, 0, 0))
    scratch = [pltpu.VMEM((2, 2, QBLK, QBLK), F32),
               pltpu.VMEM((2, 2, QBLK, QBLK), BF16),
               pltpu.VMEM((2, HEAD_DIM + DENOM_ROWS, QBLK), F32)]
    return _attn_call(_fox_kernel, "fox", (kf, ka, vft, qft, ct, cf, qn, knmax, cend),
                      [k_spec, ka_spec, vt_spec, qt_spec, ct_spec, head_spec, head_spec, knmax_spec, cend_spec],
                      scratch, kf.shape[0], lp)


def _sb_call(ks, vst, qst, tri_up, lp):
    k_spec, vt_spec, qt_spec = _pair_specs(lp)
    return _attn_call(_sb_kernel, "sb", (ks, vst, qst, tri_up),
                      [k_spec, vt_spec, qt_spec, _const_spec(tri_up.shape)], [], ks.shape[0], lp)


def _tail_kernel(x_ref, meta_ref, of_ref, os_ref, ga_ref, gb_ref, wof_ref, wos_ref, wout_ref, g1_ref,
                 g2_ref, g3_ref, wup_ref, cw_ref, cb_ref, wdn_ref, out_ref, u_ref):
    t = pl.program_id(1)

    def mix(h, rows):
        ya = jnp.dot(of_ref[0, rows, :], wof_ref[...], preferred_element_type=F32)
        yb = jnp.dot(os_ref[0, rows, :], wos_ref[...], preferred_element_type=F32)
        merged = (jax.nn.sigmoid(ga_ref[0, rows, :].astype(F32)) * ya
                  + jax.nn.sigmoid(gb_ref[0, rows, :].astype(F32)) * yb)
        mixed = jnp.dot(merged.astype(BF16), wout_ref[...], preferred_element_type=F32)
        return h + _rms_norm(mixed, g1_ref[...])

    def up_projection(h1):
        xn = _rms_norm(h1, g2_ref[...]).astype(BF16)
        return jnp.dot(xn, wup_ref[...], preferred_element_type=F32)

    @pl.when(t == 0)
    def _():
        meta_rows = slice(PAD, ROWS)
        u_ref[HALO + PAD:HALO + ROWS, :] = up_projection(mix(meta_ref[meta_rows, :], meta_rows))

    @pl.when(t > 0)
    def _():
        out_ref[0] = mix(x_ref[0], slice(None))
        u_ref[0:HALO, :] = u_ref[ROWS:ROWS + HALO, :]
        u_ref[HALO:HALO + ROWS, :] = up_projection(out_ref[0])

        def conv(cols):
            out = cb_ref[:, cols]
            for i in range(CONV_WIDTH):
                lo = HALO - (CONV_WIDTH - 1) + i
                out = out + cw_ref[i:i + 1, cols] * u_ref[lo:lo + ROWS, cols]
            return out

        ffn = jnp.zeros((ROWS, D_MODEL), F32)
        for j in range(D_FF // FF_CHUNK):
            gate = conv(slice(j * FF_CHUNK, (j + 1) * FF_CHUNK))
            val = conv(slice(D_FF + j * FF_CHUNK, D_FF + (j + 1) * FF_CHUNK))
            act = (jax.nn.gelu(gate, approximate=True) * val).astype(BF16)
            ffn = ffn + jnp.dot(act, wdn_ref[j * FF_CHUNK:(j + 1) * FF_CHUNK, :], preferred_element_type=F32)

        out_ref[0] = out_ref[0] + _rms_norm(ffn, g3_ref[...])


def _tail_call(x, meta_blk, o_f, o_s, ga, gb, wof, wos, wout, g1, g2, g3, wup, cw, cb, wdn, lp):
    b, seq, _ = x.shape
    nt = lp // ROWS
    row_spec = lambda w: pl.BlockSpec((1, ROWS, w), lambda bi, t: (bi, t, 0))
    consts = (wof, wos, wout, g1, g2, g3, wup, cw, cb, wdn)
    return pl.pallas_call(
        _tail_kernel,
        grid=(b, nt),
        in_specs=[_x_spec(), _const_spec(meta_blk.shape), row_spec(WIDTH), row_spec(WIDTH),
                  row_spec(D_MODEL), row_spec(D_MODEL)] + [_const_spec(a.shape) for a in consts],
        out_specs=pl.BlockSpec((1, ROWS, D_MODEL), lambda bi, t: (bi, jnp.maximum(t - 1, 0), 0)),
        out_shape=jax.ShapeDtypeStruct((b, seq, D_MODEL), F32),
        scratch_shapes=[pltpu.VMEM((HALO + ROWS, 2 * D_FF), F32)],
        compiler_params=pltpu.CompilerParams(
            dimension_semantics=("arbitrary", "arbitrary"), vmem_limit_bytes=VMEM_LIMIT),
        name="tail",
    )(x, meta_blk, o_f, o_s, ga, gb, *consts)


def kernel(x, meta_tokens, norm_gains, w_in, b_forget, w_o_fox, w_o_sb, w_out, w_up, conv_w, conv_b, w_down):
    b, seq, d = x.shape
    assert d == D_MODEL and seq % QBLK == 0 and D_FF % FF_CHUNK == 0 and (QBLK + seq) // QBLK <= LANES
    assert norm_gains.shape[0] == 1, "single-layer block"
    lp = QBLK + seq
    scale = HEAD_DIM ** -0.5

    perm = jnp.argsort(b_forget[0])
    by_head = lambda a: a.reshape(D_MODEL, N_HEADS, -1)[:, perm].reshape(D_MODEL, -1)

    w = w_in[0]
    o = 0
    parts = {}
    for name, width in (("qf", WIDTH), ("kf", WIDTH), ("vf", WIDTH), ("f", N_HEADS),
                        ("qs", WIDTH), ("ks", WIDTH), ("vs", WIDTH), ("ga", D_MODEL), ("gb", D_MODEL)):
        parts[name] = w[:, o:o + width]
        o += width
    for name in ("qf", "kf", "vf", "f"):
        parts[name] = by_head(parts[name])
    f_pad = jnp.pad(parts["f"], ((0, 0), (0, LANES - N_HEADS)))
    wn = jnp.concatenate([parts["kf"], parts["ks"], parts["ga"], parts["gb"], f_pad], axis=1).astype(BF16)
    wt = jnp.concatenate([parts["qf"] * scale, parts["vf"], parts["qs"] * scale, parts["vs"]], axis=1).T.astype(BF16)
    bfp = jnp.pad(b_forget[0][perm], (0, LANES - N_HEADS)).reshape(1, LANES)
    wof = w_o_fox[0].reshape(N_HEADS, HEAD_DIM, D_MODEL)[perm].reshape(WIDTH, D_MODEL).astype(BF16)

    meta_blk = jnp.concatenate([jnp.zeros((PAD, D_MODEL), x.dtype), meta_tokens.astype(x.dtype)], axis=0)
    gains = norm_gains[0].reshape(4, 1, D_MODEL)

    tri = lambda n, op: op(jnp.arange(n)[None, :], jnp.arange(n)[:, None]).astype(BF16)
    tri_low = tri(ROWS, jnp.less_equal)
    tri_up = tri(BLK, jnp.greater)
    ind = (jnp.arange(WIDTH)[:, None] // HEAD_DIM == jnp.arange(LANES)[None, :]).astype(BF16)
    indt = ind[:, :16].T

    (kf, ks, ga, gb, qft, vft, qst, vst, ka, ct, cf, qn, knmax, cend) = _proj_call(
        x, meta_blk, gains[0], wn, wt, bfp, tri_low, ind, indt, lp)
    cend = jnp.pad(cend[:, :, 0, :N_HEADS].transpose(0, 2, 1), ((0, 0), (0, 0), (0, LANES - lp // QBLK)))
    o_f = _fox_call(kf, ka, vft, qft, ct, cf, qn, knmax, cend, lp)
    o_s = _sb_call(ks, vst, qst, tri_up, lp)
    return _tail_call(x, meta_blk, o_f, o_s, ga, gb, wof, w_o_sb[0].astype(BF16), w_out[0].astype(BF16),
                      gains[1], gains[2], gains[3], w_up[0].astype(BF16), conv_w[0],
                      conv_b[0].reshape(1, 2 * D_FF), w_down[0].astype(BF16), lp)
```

```python
import jax
import jax.numpy as jnp
from jax import lax
from jax.experimental import pallas as pl
from jax.experimental.pallas import tpu as pltpu

D_MODEL = 1024
N_META = 16
HEAD_DIM = 64
N_HEADS = 8
WIDTH = N_HEADS * HEAD_DIM
D_FF = 2816
CONV_WIDTH = 3
EPS = 1e-6

BLK = 256
QBLK = 512
ROWS = QBLK
KPQ = QBLK // BLK
PAD = QBLK - N_META
LANES = 128
HALO = 8
FF_CHUNK = 256
NEG = -1e30
SBT = 128
SB_TILES = 3
EXP_UNDERFLOW = -104.0
PAD_BIAS = 32768.0
NORM_SLACK = 1.01
EXIT_SLACK = 2.0
SOFTMAX_ROWS = 64
DENOM_ROWS = 16
VMEM_LIMIT = 56 * 1024 * 1024

F32 = jnp.float32
BF16 = jnp.bfloat16


def _const_spec(shape):
    zeros = (0,) * len(shape)
    return pl.BlockSpec(shape, lambda *_: zeros, pipeline_mode=pl.Buffered(1))


def _rms_norm(v, g):
    ms = jnp.mean(v * v, axis=-1, keepdims=True)
    return v * lax.rsqrt(ms + EPS) * g


def _split3(v):
    hi = v.astype(BF16)
    r1 = v - hi.astype(F32)
    mid = r1.astype(BF16)
    lo = (r1 - mid.astype(F32)).astype(BF16)
    return hi, mid, lo


def _log_sigmoid(v):
    return jnp.minimum(v, 0.0) - jnp.log1p(jnp.exp(-jnp.abs(v)))


def _tile_input(t, x_ref, meta_ref):
    return jnp.where(t == 0, meta_ref[...], x_ref[0])


def _valid_rows(t, shape):
    row = lax.broadcasted_iota(jnp.int32, shape, 0)
    return (t > 0) | (row >= PAD)


def _x_spec():
    return pl.BlockSpec((1, ROWS, D_MODEL), lambda bi, t: (bi, jnp.maximum(t - 1, 0), 0))


def _proj_kernel(x_ref, meta_ref, g_ref, wn_ref, wt_ref, bf_ref, tri_ref, ind_ref, indt_ref,
                 kf_ref, ks_ref, ga_ref, gb_ref, qft_ref, vft_ref, qst_ref, vst_ref,
                 ka_ref, ct_ref, cf_ref, qn_ref, knmax_ref, cend_ref, carry_ref):
    t = pl.program_id(1)

    @pl.when(t == 0)
    def _():
        carry_ref[...] = jnp.zeros_like(carry_ref)
        knmax_ref[...] = jnp.zeros_like(knmax_ref)

    h = _tile_input(t, x_ref, meta_ref)
    xn = _rms_norm(h, g_ref[...]).astype(BF16)

    pn = jnp.dot(xn, wn_ref[...], preferred_element_type=F32)
    kf_ref[0] = pn[:, 0:WIDTH].astype(BF16)
    ks_ref[0] = pn[:, WIDTH:2 * WIDTH].astype(BF16)
    ga_ref[0] = pn[:, 2 * WIDTH:2 * WIDTH + D_MODEL].astype(BF16)
    gb_ref[0] = pn[:, 2 * WIDTH + D_MODEL:2 * WIDTH + 2 * D_MODEL].astype(BF16)

    pt = lax.dot_general(wt_ref[...], xn, (((1,), (1,)), ((), ())),
                         preferred_element_type=F32)
    qft_ref[0] = pt[0:WIDTH].astype(BF16)
    vft_ref[0] = pt[WIDTH:2 * WIDTH].astype(BF16)
    qst_ref[0] = pt[2 * WIDTH:3 * WIDTH].astype(BF16)
    vst_ref[0] = pt[3 * WIDTH:4 * WIDTH].astype(BF16)

    f = pn[:, 2 * WIDTH + 2 * D_MODEL:] + bf_ref[...]
    lane = lax.broadcasted_iota(jnp.int32, (ROWS, LANES), 1)
    valid = (lane < N_HEADS) & _valid_rows(t, (ROWS, LANES))
    logf = jnp.where(valid, _log_sigmoid(f), 0.0)
    tri = tri_ref[...]
    c = carry_ref[...]
    for part in _split3(logf):
        c = c + jnp.dot(tri, part, preferred_element_type=F32)
    carry_ref[...] = c[ROWS - 1:ROWS, :]

    chi, cmid, clo = _split3(c)
    spread = (chi.astype(F32) + pltpu.roll(cmid.astype(F32), N_HEADS, axis=1)
              + pltpu.roll(clo.astype(F32), 2 * N_HEADS, axis=1))
    ones = jnp.where((lane >= 32) & (lane < 32 + 3 * N_HEADS), 1.0, 0.0)
    pad_key = (lane < N_HEADS) & jnp.logical_not(_valid_rows(t, (ROWS, LANES)))
    ka_ref[0] = (ones - jnp.where(pad_key, PAD_BIAS, spread)).astype(BF16)
    ct_ref[0] = spread.T[0:32].astype(BF16)

    cf_ref[0] = c.T[0:N_HEADS]
    cend_ref[0, 0] = jnp.broadcast_to(c[ROWS - 1:ROWS, :], (8, LANES))
    qb = pt[0:WIDTH].astype(BF16).astype(F32)
    qn2 = jnp.dot(indt_ref[...], (qb * qb).astype(BF16), preferred_element_type=F32)
    qn_ref[0] = jnp.sqrt(qn2[0:N_HEADS]) * NORM_SLACK
    kb = pn[:, 0:WIDTH].astype(BF16).astype(F32)
    kn2 = jnp.dot((kb * kb).astype(BF16), ind_ref[...], preferred_element_type=F32)
    kn = jnp.sqrt(jnp.max(kn2, axis=0, keepdims=True)) * NORM_SLACK
    knmax_ref[0] = jnp.maximum(knmax_ref[0], kn)


def _proj_call(x, meta_blk, g0, wn, wt, bfp, tri, ind, indt, lp):
    b = x.shape[0]
    nt = lp // ROWS
    row_spec = lambda w: pl.BlockSpec((1, ROWS, w), lambda bi, t: (bi, t, 0))
    col_spec = lambda r: pl.BlockSpec((1, r, ROWS), lambda bi, t: (bi, 0, t))
    nat = lambda w: jax.ShapeDtypeStruct((b, lp, w), BF16)
    tr = lambda r, dt=BF16: jax.ShapeDtypeStruct((b, r, lp), dt)
    consts = (meta_blk, g0, wn, wt, bfp, tri, ind, indt)
    return pl.pallas_call(
        _proj_kernel,
        grid=(b, nt),
        in_specs=[_x_spec()] + [_const_spec(a.shape) for a in consts],
        out_specs=[row_spec(WIDTH), row_spec(WIDTH), row_spec(D_MODEL), row_spec(D_MODEL),
                   col_spec(WIDTH), col_spec(WIDTH), col_spec(WIDTH), col_spec(WIDTH),
                   row_spec(LANES), col_spec(32), col_spec(N_HEADS), col_spec(N_HEADS),
                   pl.BlockSpec((1, 1, LANES), lambda bi, t: (bi, 0, 0)),
                   pl.BlockSpec((1, 1, 8, LANES), lambda bi, t: (bi, t, 0, 0))],
        out_shape=[nat(WIDTH), nat(WIDTH), nat(D_MODEL), nat(D_MODEL),
                   tr(WIDTH), tr(WIDTH), tr(WIDTH), tr(WIDTH), nat(LANES), tr(32),
                   tr(N_HEADS, F32), tr(N_HEADS, F32),
                   jax.ShapeDtypeStruct((b, 1, LANES), F32),
                   jax.ShapeDtypeStruct((b, nt, 8, LANES), F32)],
        scratch_shapes=[pltpu.VMEM((1, LANES), F32)],
        compiler_params=pltpu.CompilerParams(
            dimension_semantics=("arbitrary", "arbitrary"), vmem_limit_bytes=VMEM_LIMIT),
        name="proj",
    )(x, *consts)


def _tile_mask(kb, qi, strict):
    s_idx = kb * BLK + lax.broadcasted_iota(jnp.int32, (BLK, QBLK), 0)
    t_idx = qi * QBLK + lax.broadcasted_iota(jnp.int32, (BLK, QBLK), 1)
    causal = (s_idx < t_idx) if strict else (s_idx <= t_idx)
    return causal & (s_idx >= PAD)


def _head_rows(qt2, hh):
    z = jnp.zeros((HEAD_DIM, QBLK), BF16)
    q = qt2[hh * HEAD_DIM:(hh + 1) * HEAD_DIM]
    return jnp.concatenate([q, z] if hh == 0 else [z, q], axis=0)


def _fox_kernel(k2_ref, ka_ref, vt2_ref, qt2_ref, ct_ref, cf_ref, qn_ref, knmax_ref, cend_ref,
                o_ref, s_ref, p_ref, acc_ref, smax_ref):
    refs = (k2_ref, ka_ref, vt2_ref, qt2_ref, ct_ref, cf_ref, qn_ref, knmax_ref, cend_ref,
            o_ref, s_ref, p_ref, acc_ref, smax_ref)

    def query_block(qi, carry):
        _fox_query_block(qi, *refs)
        return carry

    lax.fori_loop(0, qt2_ref.shape[2] // QBLK, query_block, 0)


def _fox_query_block(qi, k2_ref, ka_ref, vt2_ref, qt2_ref, ct_ref, cf_ref, qn_ref, knmax_ref, cend_ref,
                     o_ref, s_ref, p_ref, acc_ref, smax_ref):
    p = pl.program_id(1)
    qoff = pl.multiple_of(qi * QBLK, QBLK)
    qt2 = qt2_ref[0, :, pl.ds(qoff, QBLK)]
    ct = ct_ref[0, :, pl.ds(qoff, QBLK)]
    r32 = lax.broadcasted_iota(jnp.int32, (32, QBLK), 0)

    rhs = []
    for hh in range(2):
        sel = (r32 < 3 * N_HEADS) & ((r32 % N_HEADS) == 2 * p + hh)
        rhs.append(jnp.concatenate(
            [_head_rows(qt2, hh),
             jnp.where(sel, 1.0, 0.0).astype(BF16),
             jnp.where(sel, ct, jnp.zeros_like(ct)),
             jnp.zeros((64, QBLK), BF16)], axis=0))

    def scores(kb, diagonal, w, heads=(0, 1)):
        off = pl.multiple_of(kb * QBLK, QBLK)
        lhs = jnp.concatenate([k2_ref[0, pl.ds(off, QBLK), :], ka_ref[0, pl.ds(off, QBLK), :]], axis=1)
        if diagonal:
            ok = (lax.broadcasted_iota(jnp.int32, (QBLK, QBLK), 0)
                  <= lax.broadcasted_iota(jnp.int32, (QBLK, QBLK), 1))
        for hh in heads:
            s = jnp.dot(lhs, rhs[hh], preferred_element_type=F32)
            if diagonal:
                s = jnp.where(ok, s, NEG)
            s_ref[w, hh] = s
            smax_ref[w, hh] = jnp.max(s, axis=0, keepdims=True)

    def softmax(b, stats, heads=(0, 1)):
        out = list(stats)
        for hh in heads:
            m = stats[hh][0]
            m_new = jnp.maximum(m, smax_ref[b, hh])
            for r in range(0, QBLK, SOFTMAX_ROWS):
                pr = jnp.exp(s_ref[b, hh, r:r + SOFTMAX_ROWS, :] - m_new)
                p_ref[b, hh, r:r + SOFTMAX_ROWS, :] = pr.astype(BF16)
            out[hh] = (m_new, jnp.exp(m - m_new))
        return tuple(out)

    def values(kb, b, stats, heads=(0, 1)):
        off = pl.multiple_of(kb * QBLK, QBLK)
        for hh in heads:
            vt = vt2_ref[0, hh * HEAD_DIM:(hh + 1) * HEAD_DIM, pl.ds(off, QBLK)]
            lhs = jnp.concatenate([vt, jnp.ones((DENOM_ROWS, QBLK), BF16)], axis=0)
            acc_ref[hh] = stats[hh][1] * acc_ref[hh] + jnp.dot(lhs, p_ref[b, hh], preferred_element_type=F32)

    acc_ref[...] = jnp.zeros_like(acc_ref)
    one = (jnp.full((1, QBLK), NEG, F32), jnp.ones((1, QBLK), F32))

    def single_block(_):
        scores(0, True, 0)
        stats = softmax(0, (one, one))
        values(0, 0, stats)
        return stats

    def sweep(_):
        scores(qi, True, 0)
        scores(qi - 1, False, 1)
        stats = softmax(0, (one, one))

        lane = lax.broadcasted_iota(jnp.int32, (1, LANES), 1)
        need = jnp.float32(0.0)
        for hh in range(2):
            head = 2 * p + hh
            kn = jnp.max(jnp.where(lane == head, knmax_ref[0], 0.0), axis=1, keepdims=True)
            bound = (qn_ref[0, pl.ds(head, 1), pl.ds(qoff, QBLK)] * kn
                     + cf_ref[0, pl.ds(head, 1), pl.ds(qoff, QBLK)] - stats[hh][0])
            limit = jnp.max(bound, axis=1, keepdims=True) - EXP_UNDERFLOW + EXIT_SLACK
            live = (cend_ref[0, pl.ds(head, 1), :] <= limit) & (lane <= qi - 2)
            need = jnp.maximum(need, jnp.sum(jnp.where(live, 1.0, 0.0)))
        n_blocks = 2 + need.astype(jnp.int32)

        def body(k, cur, st):
            scores(qi - k, False, cur, (0,))
            first = softmax(1 - cur, st, (0,))
            scores(qi - k, False, cur, (1,))
            values(qi - k + 2, cur, st, (0,))
            second = softmax(1 - cur, st, (1,))
            values(qi - k + 2, cur, st, (1,))
            return first[0], second[1]

        def drain(cur, st):
            last = softmax(1 - cur, st)
            values(qi - n_blocks + 2, cur, st)
            values(qi - n_blocks + 1, 1 - cur, last)
            return last

        def by_parity(k, fn, st):
            return lax.cond(lax.bitwise_and(k, 1) == 0, lambda s: fn(0, s), lambda s: fn(1, s), st)

        stats = lax.fori_loop(2, n_blocks, lambda k, st: by_parity(k, lambda c, s: body(k, c, s), st), stats)
        return by_parity(n_blocks, drain, stats)

    lax.cond(qi == 0, single_block, sweep, 0)
    rows = [acc_ref[hh, 0:HEAD_DIM, :] / acc_ref[hh, HEAD_DIM:HEAD_DIM + 1, :] for hh in range(2)]
    o_ref[0, pl.ds(qoff, QBLK), :] = jnp.concatenate(rows, axis=0).T.astype(BF16)


def _sb_kernel(k2_ref, vt2_ref, qt2_ref, tt_ref, o_ref):
    def query_block(qi, carry):
        _sb_query_block(qi, k2_ref, vt2_ref, qt2_ref, tt_ref, o_ref)
        return carry

    lax.fori_loop(0, qt2_ref.shape[2] // QBLK, query_block, 0)


def _sb_query_block(qi, k2_ref, vt2_ref, qt2_ref, tt_ref, o_ref):
    assert KPQ == 2 and PAD // BLK == 1 and (SB_TILES - 1) * SBT == BLK
    qoff = pl.multiple_of(qi * QBLK, QBLK)
    qt2 = qt2_ref[0, :, pl.ds(qoff, QBLK)]
    rhs = [_head_rows(qt2, hh) for hh in range(2)]
    tt = tt_ref[...]
    top = KPQ * qi + 1
    nsub = QBLK // SBT

    def log_keep(z):
        return jnp.minimum(-z, 0.0) - jnp.log(1.0 + jnp.exp(-jnp.abs(z)))

    def block(z, ok, cr, tri):
        lk = log_keep(z)
        if ok is not None:
            lk = jnp.where(ok, lk, 0.0)
        lkb = lk.astype(BF16)
        later = jnp.dot(tri, lkb, preferred_element_type=F32) + cr
        a = jnp.exp((z + lk) + later)
        if ok is not None:
            a = jnp.where(ok, a, 0.0)
        return a.astype(BF16), later[0:1, :] + lkb[0:1, :].astype(F32)

    def first_block(_):
        half_ok = _tile_mask(top, qi, strict=True)[:, BLK:]
        off = pl.multiple_of(top * BLK, BLK)
        out = []
        for hh in range(2):
            z = jnp.dot(k2_ref[0, pl.ds(off, BLK), :], rhs[hh][:, BLK:], preferred_element_type=F32)
            a, cr = block(z, half_ok, jnp.zeros((1, BLK), F32), tt)
            a = jnp.concatenate([jnp.zeros((BLK, BLK), BF16), a], axis=1)
            cr = jnp.concatenate([jnp.zeros((1, BLK), F32), cr], axis=1)
            vt = vt2_ref[0, hh * HEAD_DIM:(hh + 1) * HEAD_DIM, pl.ds(off, BLK)]
            out.append((cr, jnp.dot(vt, a, preferred_element_type=F32)))
        return tuple(out)

    def window(pad_keys):
        def run(_):
            n_tiles = nsub + SB_TILES - 1
            off0 = pl.multiple_of(qi * QBLK - (SB_TILES - 1) * SBT, SBT)
            kwin = k2_ref[0, pl.ds(off0, n_tiles * SBT), :]
            row = lax.broadcasted_iota(jnp.int32, (SBT, QBLK), 0)
            col = lax.broadcasted_iota(jnp.int32, (SBT, QBLK), 1)
            causal = row < lax.bitwise_and(col, SBT - 1)
            tri = tt[:SBT, :SBT]
            zs = [jnp.dot(kwin, rhs[hh], preferred_element_type=F32) for hh in range(2)]
            crs = [jnp.zeros((1, QBLK), F32) for _ in range(2)]
            weights = [[], []]
            for age in range(SB_TILES):
                first = SB_TILES - 1 - age
                ok = causal if age == 0 else None
                if pad_keys:
                    real = off0 + first * SBT + lax.bitwise_and(col, -SBT) + row >= PAD
                    ok = real if ok is None else ok & real
                for hh in range(2):
                    z_age = jnp.concatenate(
                        [zs[hh][(first + j) * SBT:(first + j + 1) * SBT, j * SBT:(j + 1) * SBT]
                         for j in range(nsub)], axis=1)
                    a, crs[hh] = block(z_age, ok, crs[hh], tri)
                    weights[hh].append(a)
            zero = jnp.zeros((SBT, SBT), BF16)
            out = []
            for hh in range(2):
                band = jnp.concatenate(
                    [jnp.concatenate(
                        [weights[hh][SB_TILES - 1 + j - i][:, j * SBT:(j + 1) * SBT]
                         if 0 <= SB_TILES - 1 + j - i < SB_TILES else zero for j in range(nsub)], axis=1)
                     for i in range(n_tiles)], axis=0)
                vt = vt2_ref[0, hh * HEAD_DIM:(hh + 1) * HEAD_DIM, pl.ds(off0, n_tiles * SBT)]
                out.append((crs[hh], jnp.dot(vt, band, preferred_element_type=F32)))
            return tuple(out)
        return run

    state = lax.switch(jnp.minimum(qi, 2), [first_block, window(True), window(False)], 0)

    def tile(kb, mask, st):
        off = pl.multiple_of(kb * BLK, BLK)
        k2 = k2_ref[0, pl.ds(off, BLK), :]
        ok = None
        if mask == "pad":
            ok = _tile_mask(kb, qi, strict=True)
        elif mask == "before_window":
            s_idx = kb * BLK + lax.broadcasted_iota(jnp.int32, (BLK, QBLK), 0)
            col = lax.broadcasted_iota(jnp.int32, (BLK, QBLK), 1)
            lo = qi * QBLK + lax.bitwise_and(col, -SBT) - (SB_TILES - 1) * SBT
            ok = (s_idx < lo) & (s_idx >= PAD)
        out = []
        for hh in range(2):
            cr, acc = st[hh]
            a, cr = block(jnp.dot(k2, rhs[hh], preferred_element_type=F32), ok, cr, tt)
            vt = vt2_ref[0, hh * HEAD_DIM:(hh + 1) * HEAD_DIM, pl.ds(off, BLK)]
            out.append((cr, acc + jnp.dot(vt, a, preferred_element_type=F32)))
        return tuple(out)

    def live(st):
        return jnp.maximum(jnp.max(st[0][0]), jnp.max(st[1][0])) >= EXP_UNDERFLOW

    def older_keys(st):
        st = tile(top - 2, "before_window", tile(top - 1, "before_window", st))
        kb, st = lax.while_loop(lambda c: (c[0] >= 2) & live(c[1]),
                                lambda c: (c[0] - 1, tile(c[0], None, c[1])),
                                (top - 3, st))
        return lax.cond((kb == 1) & live(st), lambda s_: tile(1, "pad", s_), lambda s_: s_, st)

    state = lax.cond((qi >= 1) & live(state), older_keys, lambda st: st, state)
    o_ref[0, pl.ds(qoff, QBLK), :] = jnp.concatenate([state[0][1], state[1][1]], axis=0).T.astype(BF16)


def _attn_call(kernel, name, args, in_specs, scratch, b, lp):
    return pl.pallas_call(
        kernel,
        grid=(b, N_HEADS // 2),
        in_specs=in_specs,
        out_specs=pl.BlockSpec((1, lp, LANES), lambda bi, p: (bi, 0, p)),
        out_shape=jax.ShapeDtypeStruct((b, lp, WIDTH), BF16),
        scratch_shapes=scratch,
        compiler_params=pltpu.CompilerParams(
            dimension_semantics=("arbitrary", "arbitrary"), vmem_limit_bytes=VMEM_LIMIT),
        name=name,
    )(*args)


def _pair_specs(lp):
    k_spec = pl.BlockSpec((1, lp, LANES), lambda bi, p: (bi, 0, p))
    vt_spec = pl.BlockSpec((1, LANES, lp), lambda bi, p: (bi, p, 0))
    qt_spec = vt_spec
    return k_spec, vt_spec, qt_spec


def _fox_call(kf, ka, vft, qft, ct, cf, qn, knmax, cend, lp):
    k_spec, vt_spec, qt_spec = _pair_specs(lp)
    ka_spec = pl.BlockSpec((1, lp, LANES), lambda bi, p: (bi, 0, 0))
    ct_spec = pl.BlockSpec((1, 32, lp), lambda bi, p: (bi, 0, 0))
    head_spec = pl.BlockSpec((1, N_HEADS, lp), lambda bi, p: (bi, 0, 0))
    knmax_spec = pl.BlockSpec((1, 1, LANES), lambda bi, p: (bi, 0, 0))
    cend_spec = pl.BlockSpec((1, N_HEADS, LANES), lambda bi, p: (bi, 0, 0))
    scratch = [pltpu.VMEM((2, 2, QBLK, QBLK), F32),
               pltpu.VMEM((2, 2, QBLK, QBLK), BF16),
               pltpu.VMEM((2, HEAD_DIM + DENOM_ROWS, QBLK), F32),
               pltpu.VMEM((2, 2, 1, QBLK), F32)]
    return _attn_call(_fox_kernel, "fox", (kf, ka, vft, qft, ct, cf, qn, knmax, cend),
                      [k_spec, ka_spec, vt_spec, qt_spec, ct_spec, head_spec, head_spec, knmax_spec, cend_spec],
                      scratch, kf.shape[0], lp)


def _sb_call(ks, vst, qst, tri_up, lp):
    k_spec, vt_spec, qt_spec = _pair_specs(lp)
    return _attn_call(_sb_kernel, "sb", (ks, vst, qst, tri_up),
                      [k_spec, vt_spec, qt_spec, _const_spec(tri_up.shape)], [], ks.shape[0], lp)


def _tail_kernel(x_ref, meta_ref, of_ref, os_ref, ga_ref, gb_ref, wof_ref, wos_ref, wout_ref, g1_ref,
                 g2_ref, g3_ref, wup_ref, cw_ref, cb_ref, wdn_ref, out_ref, u_ref):
    t = pl.program_id(1)

    def mix(h, rows):
        ya = jnp.dot(of_ref[0, rows, :], wof_ref[...], preferred_element_type=F32)
        yb = jnp.dot(os_ref[0, rows, :], wos_ref[...], preferred_element_type=F32)
        merged = (jax.nn.sigmoid(ga_ref[0, rows, :].astype(F32)) * ya
                  + jax.nn.sigmoid(gb_ref[0, rows, :].astype(F32)) * yb)
        mixed = jnp.dot(merged.astype(BF16), wout_ref[...], preferred_element_type=F32)
        return h + _rms_norm(mixed, g1_ref[...])

    def up_projection(h1):
        xn = _rms_norm(h1, g2_ref[...]).astype(BF16)
        return jnp.dot(xn, wup_ref[...], preferred_element_type=F32)

    @pl.when(t == 0)
    def _():
        meta_rows = slice(PAD, ROWS)
        u_ref[HALO + PAD:HALO + ROWS, :] = up_projection(mix(meta_ref[meta_rows, :], meta_rows))

    @pl.when(t > 0)
    def _():
        out_ref[0] = mix(x_ref[0], slice(None))
        u_ref[0:HALO, :] = u_ref[ROWS:ROWS + HALO, :]
        u_ref[HALO:HALO + ROWS, :] = up_projection(out_ref[0])

        def conv(cols):
            out = cb_ref[:, cols]
            for i in range(CONV_WIDTH):
                lo = HALO - (CONV_WIDTH - 1) + i
                out = out + cw_ref[i:i + 1, cols] * u_ref[lo:lo + ROWS, cols]
            return out

        ffn = jnp.zeros((ROWS, D_MODEL), F32)
        for j in range(D_FF // FF_CHUNK):
            gate = conv(slice(j * FF_CHUNK, (j + 1) * FF_CHUNK))
            val = conv(slice(D_FF + j * FF_CHUNK, D_FF + (j + 1) * FF_CHUNK))
            act = (jax.nn.gelu(gate, approximate=True) * val).astype(BF16)
            ffn = ffn + jnp.dot(act, wdn_ref[j * FF_CHUNK:(j + 1) * FF_CHUNK, :], preferred_element_type=F32)

        out_ref[0] = out_ref[0] + _rms_norm(ffn, g3_ref[...])


def _tail_call(x, meta_blk, o_f, o_s, ga, gb, wof, wos, wout, g1, g2, g3, wup, cw, cb, wdn, lp):
    b, seq, _ = x.shape
    nt = lp // ROWS
    row_spec = lambda w: pl.BlockSpec((1, ROWS, w), lambda bi, t: (bi, t, 0))
    consts = (wof, wos, wout, g1, g2, g3, wup, cw, cb, wdn)
    return pl.pallas_call(
        _tail_kernel,
        grid=(b, nt),
        in_specs=[_x_spec(), _const_spec(meta_blk.shape), row_spec(WIDTH), row_spec(WIDTH),
                  row_spec(D_MODEL), row_spec(D_MODEL)] + [_const_spec(a.shape) for a in consts],
        out_specs=pl.BlockSpec((1, ROWS, D_MODEL), lambda bi, t: (bi, jnp.maximum(t - 1, 0), 0)),
        out_shape=jax.ShapeDtypeStruct((b, seq, D_MODEL), F32),
        scratch_shapes=[pltpu.VMEM((HALO + ROWS, 2 * D_FF), F32)],
        compiler_params=pltpu.CompilerParams(
            dimension_semantics=("arbitrary", "arbitrary"), vmem_limit_bytes=VMEM_LIMIT),
        name="tail",
    )(x, meta_blk, o_f, o_s, ga, gb, *consts)


def kernel(x, meta_tokens, norm_gains, w_in, b_forget, w_o_fox, w_o_sb, w_out, w_up, conv_w, conv_b, w_down):
    b, seq, d = x.shape
    assert d == D_MODEL and seq % QBLK == 0 and D_FF % FF_CHUNK == 0 and (QBLK + seq) // QBLK <= LANES
    assert norm_gains.shape[0] == 1, "single-layer block"
    lp = QBLK + seq
    scale = HEAD_DIM ** -0.5

    perm = jnp.argsort(b_forget[0])
    by_head = lambda a: a.reshape(D_MODEL, N_HEADS, -1)[:, perm].reshape(D_MODEL, -1)

    w = w_in[0]
    o = 0
    parts = {}
    for name, width in (("qf", WIDTH), ("kf", WIDTH), ("vf", WIDTH), ("f", N_HEADS),
                        ("qs", WIDTH), ("ks", WIDTH), ("vs", WIDTH), ("ga", D_MODEL), ("gb", D_MODEL)):
        parts[name] = w[:, o:o + width]
        o += width
    for name in ("qf", "kf", "vf", "f"):
        parts[name] = by_head(parts[name])
    f_pad = jnp.pad(parts["f"], ((0, 0), (0, LANES - N_HEADS)))
    wn = jnp.concatenate([parts["kf"], parts["ks"], parts["ga"], parts["gb"], f_pad], axis=1).astype(BF16)
    wt = jnp.concatenate([parts["qf"] * scale, parts["vf"], parts["qs"] * scale, parts["vs"]], axis=1).T.astype(BF16)
    bfp = jnp.pad(b_forget[0][perm], (0, LANES - N_HEADS)).reshape(1, LANES)
    wof = w_o_fox[0].reshape(N_HEADS, HEAD_DIM, D_MODEL)[perm].reshape(WIDTH, D_MODEL).astype(BF16)

    meta_blk = jnp.concatenate([jnp.zeros((PAD, D_MODEL), x.dtype), meta_tokens.astype(x.dtype)], axis=0)
    gains = norm_gains[0].reshape(4, 1, D_MODEL)

    tri = lambda n, op: op(jnp.arange(n)[None, :], jnp.arange(n)[:, None]).astype(BF16)
    tri_low = tri(ROWS, jnp.less_equal)
    tri_up = tri(BLK, jnp.greater)
    ind = (jnp.arange(WIDTH)[:, None] // HEAD_DIM == jnp.arange(LANES)[None, :]).astype(BF16)
    indt = ind[:, :16].T

    (kf, ks, ga, gb, qft, vft, qst, vst, ka, ct, cf, qn, knmax, cend) = _proj_call(
        x, meta_blk, gains[0], wn, wt, bfp, tri_low, ind, indt, lp)
    cend = jnp.pad(cend[:, :, 0, :N_HEADS].transpose(0, 2, 1), ((0, 0), (0, 0), (0, LANES - lp // QBLK)))
    o_f = _fox_call(kf, ka, vft, qft, ct, cf, qn, knmax, cend, lp)
    o_s = _sb_call(ks, vst, qst, tri_up, lp)
    return _tail_call(x, meta_blk, o_f, o_s, ga, gb, wof, w_o_sb[0].astype(BF16), w_out[0].astype(BF16),
                      gains[1], gains[2], gains[3], w_up[0].astype(BF16), conv_w[0],
                      conv_b[0].reshape(1, 2 * D_FF), w_down[0].astype(BF16), lp)
```
